```python
import math
import jax, jax.numpy as jnp
from jax import lax
import numpy as np

D_MODEL = 1024
BATCH = 32
SEQ = 256
DEPTH = 4
DEC_BATCH = 4
DEC_SEQ = 2048
PAST_LEN = 512

GRID_W = 64
HEAD_DIM = 64
NA_HEADS = 4
NA_WIN_ROWS = 8
NA_WIN_COLS = 16
NA_KEY_COLS = 2 * NA_WIN_COLS
SW_HEADS = 8
SW_KV_HEADS = 2
SW_WINDOW = 128
DA_HEADS = 4
DA_DIM = 32
DA_VDIM = 2 * DA_DIM
QBLK = 128
D_FF = 2816
CONV_W = 3
ROPE_BASE = 10000.0
EPS = 1e-6
NA_W = NA_HEADS * HEAD_DIM
SW_QW = SW_HEADS * HEAD_DIM
SW_KVW = SW_KV_HEADS * HEAD_DIM
DA_QW = DA_HEADS * 2 * DA_DIM
DA_VW = DA_HEADS * DA_VDIM
D_MIX = NA_W + SW_QW + DA_VW
D_IN = 3 * NA_W + SW_QW + 2 * SW_KVW + 2 * DA_QW + DA_VW

kernel_name = "hybrid_dit_natten_swa_diffattn_step"

F32 = jnp.float32


def rmsnorm(x, g):
    xf = x.astype(F32)
    y = xf * lax.rsqrt(jnp.mean(xf * xf, axis=-1, keepdims=True) + EPS)
    return (y * g.astype(F32)).astype(x.dtype)


def modulation(cvec, w_mod_l, b_mod_l):
    m = jax.nn.silu(cvec) @ w_mod_l + b_mod_l
    return jnp.split(m, 6, axis=-1)


def adaln(x, g, shift, scale):
    return rmsnorm(x, g) * (1 + scale) + shift


def axial_rope(T, dim):
    n = dim // 4
    inv = 1.0 / (ROPE_BASE ** (jnp.arange(n, dtype=F32) / n))
    t = jnp.arange(T)
    row = (t // GRID_W).astype(F32)
    col = (t % GRID_W).astype(F32)
    ang = jnp.concatenate([row[:, None] * inv, col[:, None] * inv], axis=-1)
    return jnp.cos(ang), jnp.sin(ang)


def apply_rope(x, cos, sin):
    shp = x.shape
    xf = x.reshape(shp[0], shp[1], -1, shp[-1]).astype(F32)
    x1, x2 = xf[..., 0::2], xf[..., 1::2]
    c = cos[None, :, None, :]
    s = sin[None, :, None, :]
    out = jnp.stack([x1 * c - x2 * s, x1 * s + x2 * c], axis=-1)
    return out.reshape(shp).astype(x.dtype)


def softmax_with_sink(s, sink_b):
    sc = jnp.broadcast_to(sink_b, s.shape[:-1] + (1,))
    p = jax.nn.softmax(jnp.concatenate([s, sc], axis=-1), axis=-1)
    return p[..., :-1]


def split_qkv(h, w_in_l, na_g, sw_g, da_g):
    B, T, _ = h.shape
    sizes = (NA_W, NA_W, NA_W, SW_QW, SW_KVW, SW_KVW, DA_QW, DA_QW, DA_VW)
    cuts = [int(i) for i in np.cumsum(sizes)[:-1]]
    p = jnp.split(h @ w_in_l, cuts, axis=-1)
    qa = rmsnorm(p[0].reshape(B, T, NA_HEADS, HEAD_DIM), na_g[0])
    ka = rmsnorm(p[1].reshape(B, T, NA_HEADS, HEAD_DIM), na_g[1])
    va = p[2].reshape(B, T, NA_HEADS, HEAD_DIM)
    qb = rmsnorm(p[3].reshape(B, T, SW_HEADS, HEAD_DIM), sw_g[0])
    kb = rmsnorm(p[4].reshape(B, T, SW_KV_HEADS, HEAD_DIM), sw_g[1])
    vb = p[5].reshape(B, T, SW_KV_HEADS, HEAD_DIM)
    qc = rmsnorm(p[6].reshape(B, T, DA_HEADS, 2, DA_DIM), da_g[0])
    kc = rmsnorm(p[7].reshape(B, T, DA_HEADS, 2, DA_DIM), da_g[1])
    vc = p[8].reshape(B, T, DA_HEADS, DA_VDIM)
    return qa, ka, va, qb, kb, vb, qc, kc, vc


def dense_attn(q, k, v, sink=None):
    B, T, HQ, D = q.shape
    HKV = k.shape[2]
    G = HQ // HKV
    DV = v.shape[-1]
    nb = T // QBLK
    qb = jnp.moveaxis(q.reshape(B, nb, QBLK, HKV, G, D), 1, 0)
    scale = D ** -0.5

    def one(qblk):
        s = jnp.einsum('bqhgd,bkhd->bhgqk', qblk, k).astype(F32) * scale
        if sink is None:
            p = jax.nn.softmax(s, axis=-1)
        else:
            p = softmax_with_sink(s, sink.reshape(HKV, G)[None, :, :, None, None].astype(F32))
        return jnp.einsum('bhgqk,bkhd->bqhgd', p.astype(v.dtype), v)

    o = lax.map(one, qb)
    return jnp.moveaxis(o, 0, 1).reshape(B, T, HQ, DV)


def diff_lambda(lp, lam_init):
    lpf = lp.astype(F32)
    return jnp.exp(jnp.sum(lpf[0] * lpf[1])) - jnp.exp(jnp.sum(lpf[2] * lpf[3])) + lam_init


def diff_attn(q, k, v, lam, subln_g, lam_init):
    o1 = dense_attn(q[..., 0, :], k[..., 0, :], v).astype(F32)
    o2 = dense_attn(q[..., 1, :], k[..., 1, :], v).astype(F32)
    o = rmsnorm(o1 - lam * o2, subln_g) * (1.0 - lam_init)
    return o.astype(v.dtype)


def neighborhood_attn(q, k, v, kc, vc, rpb):
    B, T, H, D = q.shape
    rows = T // GRID_W
    kr = min(NA_WIN_ROWS, rows)
    ncb = GRID_W // NA_WIN_COLS
    r = jnp.arange(rows)
    r0 = jnp.clip(r - kr // 2, 0, rows - kr)
    ridx = r0[:, None] + jnp.arange(kr)[None, :]
    roff = ridx - r[:, None] + (NA_WIN_ROWS - 1)
    cb = jnp.arange(ncb)
    c0 = jnp.clip(cb * NA_WIN_COLS - NA_WIN_COLS // 2, 0, GRID_W - NA_KEY_COLS)
    cidx = c0[:, None] + jnp.arange(NA_KEY_COLS)[None, :]
    qcol = cb[:, None] * NA_WIN_COLS + jnp.arange(NA_WIN_COLS)[None, :]
    qs = jnp.clip(qcol - NA_WIN_COLS // 2, 0, GRID_W - NA_WIN_COLS)
    kcol = cidx[:, None, :]
    cmask = (kcol >= qs[..., None]) & (kcol < qs[..., None] + NA_WIN_COLS)
    coff = jnp.clip(kcol - qcol[..., None], -(NA_WIN_COLS - 1), NA_WIN_COLS - 1) + (NA_WIN_COLS - 1)
    qg = q.reshape(B, rows, ncb, NA_WIN_COLS, H, D)
    gi = (ridx[:, None, :, None], cidx[None, :, None, :])
    kg = k.reshape(B, rows, GRID_W, H, D)[:, gi[0], gi[1]]
    vg = v.reshape(B, rows, GRID_W, H, D)[:, gi[0], gi[1]]
    scale = D ** -0.5
    s_loc = jnp.einsum('brcqhd,brcakhd->brchqak', qg, kg).astype(F32) * scale
    bias = rpb[:, roff[:, None, None, :, None], coff[None, :, :, None, :]]
    s_loc = s_loc + jnp.moveaxis(bias, 0, 2).astype(F32)
    s_loc = jnp.where(cmask[None, None, :, None, :, None, :], s_loc, -jnp.inf)
    nl = kr * NA_KEY_COLS
    s_loc = s_loc.reshape(B, rows, ncb, H, NA_WIN_COLS, nl)
    s_ctx = jnp.einsum('brcqhd,bphd->brchqp', qg, kc).astype(F32) * scale
    p = jax.nn.softmax(jnp.concatenate([s_loc, s_ctx], axis=-1), axis=-1).astype(v.dtype)
    p_loc = p[..., :nl].reshape(B, rows, ncb, H, NA_WIN_COLS, kr, NA_KEY_COLS)
    o = (jnp.einsum('brchqak,brcakhd->brcqhd', p_loc, vg)
         + jnp.einsum('brchqp,bphd->brcqhd', p[..., nl:], vc))
    return o.reshape(B, T, H, D)


def window_attn(q, k, v, kc, vc, sink):
    B, T, HQ, D = q.shape
    HKV = k.shape[2]
    G = HQ // HKV
    nb = T // QBLK
    qb = q.reshape(B, nb, QBLK, HKV, G, D)
    pad = ((0, 0), (QBLK, QBLK), (0, 0), (0, 0))
    kp = jnp.pad(k, pad).reshape(B, nb + 2, QBLK, HKV, D)
    vp = jnp.pad(v, pad).reshape(B, nb + 2, QBLK, HKV, D)
    kband = jnp.concatenate([kp[:, :-2], kp[:, 1:-1], kp[:, 2:]], axis=2)
    vband = jnp.concatenate([vp[:, :-2], vp[:, 1:-1], vp[:, 2:]], axis=2)
    qi = jnp.arange(QBLK)
    kj = jnp.arange(3 * QBLK)
    rel = kj[None, :] - QBLK - qi[:, None]
    kpos = jnp.arange(nb)[:, None] * QBLK - QBLK + kj[None, :]
    mask = (jnp.abs(rel) <= SW_WINDOW)[None] & ((kpos >= 0) & (kpos < T))[:, None, :]
    scale = D ** -0.5
    s_loc = jnp.einsum('bnqhgd,bnkhd->bnhgqk', qb, kband).astype(F32) * scale
    s_loc = jnp.where(mask[None, :, None, None], s_loc, -jnp.inf)
    s_ctx = jnp.einsum('bnqhgd,bphd->bnhgqp', qb, kc).astype(F32) * scale
    p = softmax_with_sink(jnp.concatenate([s_loc, s_ctx], axis=-1),
                          sink.reshape(HKV, G)[None, None, :, :, None, None].astype(F32)).astype(v.dtype)
    nl = 3 * QBLK
    o = (jnp.einsum('bnhgqk,bnkhd->bnqhgd', p[..., :nl], vband)
         + jnp.einsum('bnhgqp,bphd->bnqhgd', p[..., nl:], vc))
    return o.reshape(B, T, HQ, D)


def merge_heads(oa, ob, oc, w_out_l):
    B, T = oa.shape[:2]
    o = jnp.concatenate([oa.reshape(B, T, NA_W), ob.reshape(B, T, SW_QW), oc.reshape(B, T, DA_VW)], axis=-1)
    return o @ w_out_l


def conv_ffn(h, w_up_l, cw, cb, w_down_l):
    u = h @ w_up_l
    T = u.shape[1]
    half = CONV_W // 2
    up = jnp.pad(u, ((0, 0), (half, half), (0, 0)))
    y = cb
    for j in range(CONV_W):
        y = y + up[:, j:j + T] * cw[j]
    g, val = jnp.split(y, 2, axis=-1)
    return (jax.nn.silu(g) * val) @ w_down_l


def setup_inputs(seed: int = 0) -> dict:
    key = jax.random.key(seed)
    ks = jax.random.split(key, 27)
    n = jax.random.normal
    d = {}
    d["x_prompt"] = n(ks[0], (BATCH, SEQ, D_MODEL), F32)
    d["x_sample"] = n(ks[1], (DEC_BATCH, DEC_SEQ, D_MODEL), F32)
    d["cache_na_k"] = n(ks[2], (DEC_BATCH, DEPTH, PAST_LEN, NA_HEADS, HEAD_DIM), F32)
    d["cache_na_v"] = n(ks[3], (DEC_BATCH, DEPTH, PAST_LEN, NA_HEADS, HEAD_DIM), F32)
    d["cache_sw_k"] = n(ks[4], (DEC_BATCH, DEPTH, PAST_LEN, SW_KV_HEADS, HEAD_DIM), F32)
    d["cache_sw_v"] = n(ks[5], (DEC_BATCH, DEPTH, PAST_LEN, SW_KV_HEADS, HEAD_DIM), F32)
    d["cache_da_k"] = n(ks[6], (DEC_BATCH, DEPTH, PAST_LEN, DA_HEADS, 2, DA_DIM), F32)
    d["cache_da_v"] = n(ks[7], (DEC_BATCH, DEPTH, PAST_LEN, DA_HEADS, DA_VDIM), F32)
    d["c"] = n(ks[8], (DEC_BATCH, D_MODEL), F32)
    d["c_ctx"] = n(ks[9], (D_MODEL,), F32)
    d["g_attn"] = 1.0 + 0.1 * n(ks[10], (DEPTH, D_MODEL), F32)
    d["g_ffn"] = 1.0 + 0.1 * n(ks[11], (DEPTH, D_MODEL), F32)
    d["w_mod"] = 0.5 * D_MODEL ** -0.5 * n(ks[12], (DEPTH, D_MODEL, 6 * D_MODEL), F32)
    d["b_mod"] = 0.02 * n(ks[13], (DEPTH, 6 * D_MODEL), F32)
    d["w_in"] = D_MODEL ** -0.5 * n(ks[14], (DEPTH, D_MODEL, D_IN), F32)
    d["na_qk_g"] = 1.0 + 0.1 * n(ks[15], (DEPTH, 2, HEAD_DIM), F32)
    d["na_rpb"] = 0.2 * n(ks[16], (DEPTH, NA_HEADS, 2 * NA_WIN_ROWS - 1, 2 * NA_WIN_COLS - 1), F32)
    d["sw_qk_g"] = 1.0 + 0.1 * n(ks[17], (DEPTH, 2, HEAD_DIM), F32)
    d["sw_sink"] = 0.5 * n(ks[18], (DEPTH, SW_HEADS), F32)
    d["da_qk_g"] = 1.0 + 0.1 * n(ks[19], (DEPTH, 2, DA_DIM), F32)
    d["da_lambda"] = 0.1 * n(ks[20], (DEPTH, 4, DA_DIM), F32)
    d["da_subln_g"] = 1.0 + 0.1 * n(ks[21], (DEPTH, DA_VDIM), F32)
    d["w_out"] = D_MIX ** -0.5 * n(ks[22], (DEPTH, D_MIX, D_MODEL), F32)
    d["w_up"] = D_MODEL ** -0.5 * n(ks[23], (DEPTH, D_MODEL, 2 * D_FF), F32)
    d["conv_w"] = CONV_W ** -0.5 * n(ks[24], (DEPTH, CONV_W, 2 * D_FF), F32)
    d["conv_b"] = 0.02 * n(ks[25], (DEPTH, 2 * D_FF), F32)
    d["w_down"] = D_FF ** -0.5 * n(ks[26], (DEPTH, D_FF, D_MODEL), F32)
    return d


def reference(x_prompt, x_sample, cache_na_k, cache_na_v, cache_sw_k, cache_sw_v, cache_da_k, cache_da_v,
              c, c_ctx, g_attn, g_ffn, w_mod, b_mod, w_in, na_qk_g, na_rpb, sw_qk_g, sw_sink,
              da_qk_g, da_lambda, da_subln_g, w_out, w_up, conv_w, conv_b, w_down):
    T = x_sample.shape[1]
    cos_b, sin_b = axial_rope(T, HEAD_DIM)
    cos_c, sin_c = axial_rope(T, DA_DIM)
    xp = x_prompt
    xs = x_sample
    na_k_l, na_v_l, sw_k_l, sw_v_l, da_k_l, da_v_l = [], [], [], [], [], []
    for l in range(DEPTH):
        lam_init = 0.8 - 0.6 * math.exp(-0.3 * l)
        lam = diff_lambda(da_lambda[l], lam_init)
        m = modulation(c_ctx[None, None, :], w_mod[l], b_mod[l])
        h = adaln(xp, g_attn[l], m[0], m[1])
        qa, ka, va, qb, kb, vb, qc, kc, vc = split_qkv(h, w_in[l], na_qk_g[l], sw_qk_g[l], da_qk_g[l])
        oa = dense_attn(qa, ka, va)
        ob = dense_attn(qb, kb, vb, sw_sink[l])
        oc = diff_attn(qc, kc, vc, lam, da_subln_g[l], lam_init)
        xp = xp + m[2] * merge_heads(oa, ob, oc, w_out[l])
        h = adaln(xp, g_ffn[l], m[3], m[4])
        xp = xp + m[5] * conv_ffn(h, w_up[l], conv_w[l], conv_b[l], w_down[l])
        na_k_l.append(ka)
        na_v_l.append(va)
        sw_k_l.append(kb)
        sw_v_l.append(vb)
        da_k_l.append(kc)
        da_v_l.append(vc)
        m = modulation(c[:, None, :], w_mod[l], b_mod[l])
        h = adaln(xs, g_attn[l], m[0], m[1])
        qa, ka, va, qb, kb, vb, qc, kc, vc = split_qkv(h, w_in[l], na_qk_g[l], sw_qk_g[l], da_qk_g[l])
        qb = apply_rope(qb, cos_b, sin_b)
        kb = apply_rope(kb, cos_b, sin_b)
        qc = apply_rope(qc, cos_c, sin_c)
        kc = apply_rope(kc, cos_c, sin_c)
        oa = neighborhood_attn(qa, ka, va, cache_na_k[:, l], cache_na_v[:, l], na_rpb[l])
        ob = window_attn(qb, kb, vb, cache_sw_k[:, l], cache_sw_v[:, l], sw_sink[l])
        oc = diff_attn(qc, jnp.concatenate([kc, cache_da_k[:, l]], axis=1),
                       jnp.concatenate([vc, cache_da_v[:, l]], axis=1), lam, da_subln_g[l], lam_init)
        xs = xs + m[2] * merge_heads(oa, ob, oc, w_out[l])
        h = adaln(xs, g_ffn[l], m[3], m[4])
        xs = xs + m[5] * conv_ffn(h, w_up[l], conv_w[l], conv_b[l], w_down[l])
    return (xp, xs, jnp.stack(na_k_l, axis=1), jnp.stack(na_v_l, axis=1), jnp.stack(sw_k_l, axis=1),
            jnp.stack(sw_v_l, axis=1), jnp.stack(da_k_l, axis=1), jnp.stack(da_v_l, axis=1))
```

```python
import functools
import math

import numpy as np
import jax
import jax.numpy as jnp
from jax import lax
from jax.experimental import pallas as pl
from jax.experimental.pallas import tpu as pltpu

F32 = jnp.float32
BF16 = jnp.bfloat16

D_MODEL = 1024
DEPTH = 4
GRID_W = 64
HEAD_DIM = 64
NA_HEADS = 4
NA_WIN_ROWS = 8
NA_WIN_COLS = 16
SW_HEADS = 8
SW_KV_HEADS = 2
SW_WINDOW = 128
DA_HEADS = 4
DA_DIM = 32
D_FF = 2816
ROPE_BASE = 10000.0
EPS = 1e-6

QB = 0
QA, KA, VA = 512, 768, 1024
KB, VB = 1280, 1408
QC, KC, VC = 1536, 1792, 2048
D_IN = 2304
_REF_COLS = ((QA, 0, 256), (KA, 256, 256), (VA, 512, 256), (QB, 768, 512), (KB, 1280, 128),
             (VB, 1408, 128), (QC, 1536, 256), (KC, 1792, 256), (VC, 2048, 256))
D_MIX = 1024
_SW_ORDER = (0, 4, 1, 5, 2, 6, 3, 7)

LANES = 128
NEG = -1e30
TQ = 256
TM_IN = 512
TM_FFN = 1024
HALO = 16
CK_FFN = 256
VMEM_LIMIT = 56 * 1024 * 1024


def _cparams(sem):
    return pltpu.CompilerParams(dimension_semantics=sem, vmem_limit_bytes=VMEM_LIMIT)


def _dot(a, b):
    return jnp.dot(a, b, preferred_element_type=F32)


def _dot_nt(a, b):
    return lax.dot_general(a, b, (((1,), (1,)), ((), ())), preferred_element_type=F32)


def _sigmoid(x):
    return 1.0 / (1.0 + jnp.exp(-x))


def _mod_kernel(c_ref, w_ref, b_ref, o_ref):
    c = c_ref[...]
    s = (c * _sigmoid(c)).astype(BF16)
    o_ref[...] = _dot(s, w_ref[...].astype(BF16)) + b_ref[...]


def _modulation(cvecs, w_mod, b_mod):
    nchunk = 4
    cn = 6 * D_MODEL // nchunk
    return pl.pallas_call(
        _mod_kernel,
        grid=(DEPTH, nchunk),
        in_specs=[
            pl.BlockSpec((8, D_MODEL), lambda l, n: (0, 0)),
            pl.BlockSpec((None, D_MODEL, cn), lambda l, n: (l, 0, n)),
            pl.BlockSpec((None, 1, cn), lambda l, n: (l, 0, n)),
        ],
        out_specs=pl.BlockSpec((None, 8, cn), lambda l, n: (l, 0, n)),
        out_shape=jax.ShapeDtypeStruct((DEPTH, 8, 6 * D_MODEL), F32),
        compiler_params=_cparams(("arbitrary", "arbitrary")),
        name="modulation",
    )(cvecs, w_mod, b_mod.reshape(DEPTH, 1, 6 * D_MODEL))


N_RPB_COLS = 2 * NA_WIN_COLS - 1
_BIAS_KINDS = (
    (3, lambda i, j: j >= i),
    (7, lambda i, j: True),
    (11, lambda i, j: j < i),
    (3, lambda i, j: True),
    (11, lambda i, j: True),
    (0, lambda i, j: False),
)
K_PREV, K_CUR, K_NEXT, K_PREV_FULL, K_NEXT_FULL, K_NONE = range(6)


def _bias_kernel(rpb_ref, out_ref):
    lh = pl.program_id(0)
    qcol = lax.broadcasted_iota(jnp.int32, (GRID_W, LANES), 0)
    lane = lax.broadcasted_iota(jnp.int32, (GRID_W, LANES), 1)
    kcol = lane & (GRID_W - 1)
    hi = lane >= GRID_W
    f = jnp.clip(kcol - qcol, -(NA_WIN_COLS - 1), NA_WIN_COLS - 1) + (NA_WIN_COLS - 1)
    qs = jnp.clip(qcol - NA_WIN_COLS // 2, 0, GRID_W - NA_WIN_COLS)
    cvalid = (kcol >= qs) & (kcol < qs + NA_WIN_COLS)
    masked = jnp.full((GRID_W, LANES), NEG, F32)

    def block_pair(a0, a1):
        if a0 is None and a1 is None:
            return masked
        acc = jnp.zeros((GRID_W, LANES), F32)
        for b in range(N_RPB_COLS):
            s0 = rpb_ref[lh, a0 * N_RPB_COLS + b] if a0 is not None else 0.0
            s1 = rpb_ref[lh, a1 * N_RPB_COLS + b] if a1 is not None else 0.0
            acc = jnp.where(f == b, jnp.where(hi, s1, s0), acc)
        valid = cvalid
        if a0 is None:
            valid = valid & hi
        if a1 is None:
            valid = valid & jnp.logical_not(hi)
        return jnp.where(valid, acc, masked)

    for kind, (off, inside) in enumerate(_BIAS_KINDS):
        for i in range(4):
            for jt in range(2):
                a = [(j - i + off) if inside(i, j) else None for j in (2 * jt, 2 * jt + 1)]
                out_ref[kind, i * GRID_W:(i + 1) * GRID_W, jt * LANES:(jt + 1) * LANES] = block_pair(a[0], a[1])


def _na_bias_tiles(na_rpb):
    n = DEPTH * NA_HEADS
    rpb2 = na_rpb.reshape(n, (2 * NA_WIN_ROWS - 1) * N_RPB_COLS)
    out = pl.pallas_call(
        _bias_kernel,
        grid=(n,),
        in_specs=[pl.BlockSpec(memory_space=pltpu.SMEM)],
        out_specs=pl.BlockSpec((None, len(_BIAS_KINDS), TQ, TQ), lambda i: (i, 0, 0, 0)),
        out_shape=jax.ShapeDtypeStruct((n, len(_BIAS_KINDS), TQ, TQ), F32),
        compiler_params=_cparams(("arbitrary",)),
        name="na_bias_tiles",
    )(rpb2)
    return out.reshape(DEPTH, NA_HEADS, len(_BIAS_KINDS), TQ, TQ)


_NORM64 = ((QB, 256), (QB + 256, 256), (QA, 256), (KA, 256), (KB, 128))
_NORM32 = ((QC, 256), (KC, 256))
_PLAIN = ((VA, 256), (VB, 128), (VC, 256))
_ROPE_B = (QB, QB + 256, KB)
_ROPE_C = (QC, KC)
_CACHE_COLS = (KA, VA, KB, VB, KC, VC)


def _inproj_kernel(*refs, rope, emit_cache):
    x_ref, mod_ref, g_ref, w_ref, gvec_ref, bd64_ref, bd32_ref = refs[:7]
    rest = refs[7:]
    if rope:
        cosb_ref, sinb_ref, cosc_ref, sinc_ref = rest[:4]
        rest = rest[4:]
    qkv_ref = rest[0]
    cache_refs = rest[1:]

    x = x_ref[...]
    ms = jnp.mean(x * x, axis=-1, keepdims=True)
    y = x * lax.rsqrt(ms + EPS) * g_ref[...]
    h = (y * (1.0 + mod_ref[:, D_MODEL:2 * D_MODEL]) + mod_ref[:, 0:D_MODEL]).astype(BF16)
    tm = x.shape[0]
    even = (lax.broadcasted_iota(jnp.int32, (tm, LANES), 1) & 1) == 0

    def rotary(t, cos_ref, sin_ref):
        cos = cos_ref[...]
        sin = sin_ref[...]
        parts = []
        for c in range(t.shape[1] // LANES):
            tc = t[:, c * LANES:(c + 1) * LANES]
            partner = jnp.where(even, pltpu.roll(tc, LANES - 1, 1), pltpu.roll(tc, 1, 1))
            parts.append(tc * cos + partner * sin)
        return parts[0] if len(parts) == 1 else jnp.concatenate(parts, axis=1)

    cache_of = dict(zip(_CACHE_COLS, cache_refs)) if emit_cache else {}

    def emit(a, w, t):
        qkv_ref[:, a:a + w] = t.astype(BF16)
        if a in cache_of:
            cache_of[a][...] = t

    for segs, bd_ref in ((_NORM64, bd64_ref), (_NORM32, bd32_ref)):
        for a, w in segs:
            t = _dot(h, w_ref[:, a:a + w])
            msq = _dot((t * t).astype(BF16), bd_ref[0:w, 0:w])
            t = t * lax.rsqrt(msq + EPS) * gvec_ref[:, a:a + w]
            if rope and a in _ROPE_B:
                t = rotary(t, cosb_ref, sinb_ref)
            if rope and a in _ROPE_C:
                t = rotary(t, cosc_ref, sinc_ref)
            emit(a, w, t)
    for a, w in _PLAIN:
        emit(a, w, _dot(h, w_ref[:, a:a + w]))


def _inproj(x, mod4, mod_row_fn, l, g_attn, w_in_bf, gvec, bd64, bd32, rope_tabs, emit_cache):
    rows = x.shape[0]
    tm = TM_IN
    grid = (rows // tm,)
    rope = rope_tabs is not None
    in_specs = [
        pl.BlockSpec((tm, D_MODEL), lambda i: (i, 0)),
        pl.BlockSpec((None, None, 1, 6 * D_MODEL), lambda i: (l, mod_row_fn(i, tm), 0, 0)),
        pl.BlockSpec((None, 1, D_MODEL), lambda i: (l, 0, 0)),
        pl.BlockSpec((None, D_MODEL, D_IN), lambda i: (l, 0, 0)),
        pl.BlockSpec((None, 1, D_IN), lambda i: (l, 0, 0)),
        pl.BlockSpec((256, 256), lambda i: (0, 0)),
        pl.BlockSpec((256, 256), lambda i: (0, 0)),
    ]
    args = [x, mod4, g_attn.reshape(DEPTH, 1, D_MODEL), w_in_bf, gvec, bd64, bd32]
    if rope:
        nt = rope_tabs[0].shape[0] // tm
        in_specs += [pl.BlockSpec((tm, LANES), lambda i: (i % nt, 0))] * 4
        args += list(rope_tabs)
    out_shape = [jax.ShapeDtypeStruct((rows, D_IN), BF16)]
    out_specs = [pl.BlockSpec((tm, D_IN), lambda i: (i, 0))]
    if emit_cache:
        for w in (256, 256, 128, 128, 256, 256):
            out_shape.append(jax.ShapeDtypeStruct((rows, w), F32))
            out_specs.append(pl.BlockSpec((tm, w), lambda i: (i, 0)))
    return pl.pallas_call(
        functools.partial(_inproj_kernel, rope=rope, emit_cache=emit_cache),
        grid=grid,
        in_specs=in_specs,
        out_specs=out_specs,
        out_shape=out_shape,
        compiler_params=_cparams(("parallel",)),
        name="inproj_lat" if rope else "inproj_ctx",
    )(*args)


def _lane_masks():
    lane = lax.broadcasted_iota(jnp.int32, (1, LANES), 1)
    lo = lane < HEAD_DIM
    quarters = [(lane >= DA_DIM * j) & (lane < DA_DIM * (j + 1)) for j in range(4)]
    return lo, quarters


def _keep(q, mask):
    return jnp.where(mask, q, jnp.zeros_like(q))


def _attend(qm, ks, vs, biases=None, sink=None):
    ss = []
    for n, k in enumerate(ks):
        s = _dot_nt(qm, k)
        if biases is not None and biases[n] is not None:
            s = s + biases[n]
        ss.append(s)
    m = ss[0].max(axis=-1, keepdims=True)
    for s in ss[1:]:
        m = jnp.maximum(m, s.max(axis=-1, keepdims=True))
    if sink is not None:
        m = jnp.maximum(m, sink)
    l = None
    o = None
    for s, v in zip(ss, vs):
        p = jnp.exp(s - m)
        ls = p.sum(axis=-1, keepdims=True)
        os = _dot(p.astype(BF16), v)
        l = ls if l is None else l + ls
        o = os if o is None else o + os
    if sink is not None:
        l = l + jnp.exp(sink - m)
    return o * (1.0 / l)


def _diff_lambda(lam_ref, lam_init):
    lp = lam_ref[...]
    a = jnp.sum(lp[0:1] * lp[1:2], axis=-1, keepdims=True)
    b = jnp.sum(lp[2:3] * lp[3:4], axis=-1, keepdims=True)
    return jnp.exp(a) - jnp.exp(b) + lam_init


def _diff_pair(q, ks, vs, lam, subg, bd, lam_init, lo, quarters):
    o = [_attend(_keep(q, quarters[j]), ks, vs) for j in range(4)]
    d = jnp.where(lo, o[0] - lam * o[1], o[2] - lam * o[3])
    msq = _dot((d * d).astype(BF16), bd)
    return d * lax.rsqrt(msq + EPS) * subg * (1.0 - lam_init)


def _attn_ctx_kernel(qkv_ref, sink_ref, lam_ref, subg_ref, bd_ref, oa_ref, ob_ref, oc_ref, *, lam_init):
    lo, quarters = _lane_masks()
    hi = jnp.logical_not(lo)
    for pr in range(2):
        c = pr * LANES
        q = qkv_ref[:, QA + c:QA + c + LANES]
        k = [qkv_ref[:, KA + c:KA + c + LANES]]
        v = [qkv_ref[:, VA + c:VA + c + LANES]]
        o = jnp.where(lo, _attend(_keep(q, lo), k, v), _attend(_keep(q, hi), k, v))
        oa_ref[:, c:c + LANES] = o.astype(BF16)
    k = [qkv_ref[:, KB:KB + LANES]]
    v = [qkv_ref[:, VB:VB + LANES]]
    for pr in range(4):
        c = pr * LANES
        q = qkv_ref[:, QB + c:QB + c + LANES]
        o0 = _attend(_keep(q, lo), k, v, sink=sink_ref[_SW_ORDER[2 * pr]])
        o1 = _attend(_keep(q, hi), k, v, sink=sink_ref[_SW_ORDER[2 * pr + 1]])
        ob_ref[:, c:c + LANES] = jnp.where(lo, o0, o1).astype(BF16)
    lam = _diff_lambda(lam_ref, lam_init)
    for pr in range(2):
        c = pr * LANES
        q = qkv_ref[:, QC + c:QC + c + LANES]
        k = [qkv_ref[:, KC + c:KC + c + LANES]]
        v = [qkv_ref[:, VC + c:VC + c + LANES]]
        o = _diff_pair(q, k, v, lam, subg_ref[...], bd_ref[...], lam_init, lo, quarters)
        oc_ref[:, c:c + LANES] = o.astype(BF16)


def _attn_ctx(qkv, l, sw_sink, da_lambda, subg, bd128, lam_init):
    rows = qkv.shape[0]
    nb = rows // TQ
    return pl.pallas_call(
        functools.partial(_attn_ctx_kernel, lam_init=lam_init),
        grid=(nb,),
        in_specs=[
            pl.BlockSpec((TQ, D_IN), lambda b: (b, 0)),
            pl.BlockSpec(memory_space=pltpu.SMEM),
            pl.BlockSpec((None, 4, DA_DIM), lambda b: (l, 0, 0)),
            pl.BlockSpec((None, 1, LANES), lambda b: (l, 0, 0)),
            pl.BlockSpec((LANES, LANES), lambda b: (0, 0)),
        ],
        out_specs=[
            pl.BlockSpec((TQ, 256), lambda b: (b, 0)),
            pl.BlockSpec((TQ, 512), lambda b: (b, 0)),
            pl.BlockSpec((TQ, 256), lambda b: (b, 0)),
        ],
        out_shape=[
            jax.ShapeDtypeStruct((rows, 256), BF16),
            jax.ShapeDtypeStruct((rows, 512), BF16),
            jax.ShapeDtypeStruct((rows, 256), BF16),
        ],
        compiler_params=_cparams(("parallel",)),
        name="attn_ctx",
    )(qkv, sw_sink[l], da_lambda, subg, bd128)


NT_LAT = 2048 // TQ


def _attn_na_kernel(q_ref, kp_ref, kc_ref, kn_ref, vp_ref, vc_ref, vn_ref, ck_ref, cv_ref,
                    bp_ref, bc_ref, bn_ref, o_ref):
    lo, _ = _lane_masks()
    hi = jnp.logical_not(lo)
    ck = ck_ref[...].astype(BF16)
    cv = cv_ref[...].astype(BF16)
    for pr in range(2):
        c = pr * LANES
        sl = slice(c, c + LANES)
        q = q_ref[:, sl]
        ks = [kp_ref[:, sl], kc_ref[:, sl], kn_ref[:, sl], ck[:, sl]]
        vs = [vp_ref[:, sl], vc_ref[:, sl], vn_ref[:, sl], cv[:, sl]]
        outs = []
        for hl, mask in enumerate((lo, hi)):
            h = 2 * pr + hl
            outs.append(_attend(_keep(q, mask), ks, vs, biases=[bp_ref[h], bc_ref[h], bn_ref[h], None]))
        o_ref[:, sl] = jnp.where(lo, outs[0], outs[1]).astype(BF16)


def _attn_na(qkv, cache_k, cache_v, bias, l):
    rows = qkv.shape[0]
    nbatch = rows // 2048
    last = NT_LAT - 1

    def row(b, r):
        return b * NT_LAT + r

    def kv_specs(col):
        return [
            pl.BlockSpec((TQ, 256), lambda r, b: (row(b, jnp.maximum(r - 1, 0)), col)),
            pl.BlockSpec((TQ, 256), lambda r, b: (row(b, r), col)),
            pl.BlockSpec((TQ, 256), lambda r, b: (row(b, jnp.minimum(r + 1, last)), col)),
        ]

    def kind_prev(r):
        return jnp.where(r == 0, K_NONE, jnp.where(r == last, K_PREV_FULL, K_PREV))

    def kind_next(r):
        return jnp.where(r == 0, K_NEXT_FULL, jnp.where(r == last, K_NONE, K_NEXT))

    def bias_spec(kind_fn):
        return pl.BlockSpec((None, NA_HEADS, None, TQ, TQ), lambda r, b: (l, 0, kind_fn(r), 0, 0))

    cache_spec = pl.BlockSpec((None, None, 512, 256), lambda r, b: (b, l, 0, 0))
    return pl.pallas_call(
        _attn_na_kernel,
        grid=(NT_LAT, nbatch),
        in_specs=[pl.BlockSpec((TQ, 256), lambda r, b: (row(b, r), QA // 256))]
        + kv_specs(KA // 256) + kv_specs(VA // 256)
        + [cache_spec, cache_spec,
           bias_spec(kind_prev), bias_spec(lambda r: K_CUR), bias_spec(kind_next)],
        out_specs=pl.BlockSpec((TQ, 256), lambda r, b: (row(b, r), 0)),
        out_shape=jax.ShapeDtypeStruct((rows, 256), BF16),
        compiler_params=_cparams(("parallel", "parallel")),
        name="attn_lat_na",
    )(qkv, qkv, qkv, qkv, qkv, qkv, qkv, cache_k, cache_v, bias, bias, bias)


def _attn_sw_kernel(q_ref, kp_ref, kc_ref, kn_ref, vp_ref, vc_ref, vn_ref, ck_ref, cv_ref, sink_ref, o_ref):
    r = pl.program_id(1)
    lo, _ = _lane_masks()
    hi = jnp.logical_not(lo)
    qi = lax.broadcasted_iota(jnp.int32, (TQ, TQ), 0)
    kj = lax.broadcasted_iota(jnp.int32, (TQ, TQ), 1)
    zero = jnp.zeros((TQ, TQ), F32)
    neg = jnp.full((TQ, TQ), NEG, F32)
    bias_p = jnp.where((kj - qi >= TQ - SW_WINDOW) & (r > 0), zero, neg)
    bias_c = jnp.where(jnp.abs(kj - qi) <= SW_WINDOW, zero, neg)
    bias_n = jnp.where((kj - qi <= SW_WINDOW - TQ) & (r < NT_LAT - 1), zero, neg)
    ks = [kp_ref[...], kc_ref[...], kn_ref[...], ck_ref[...].astype(BF16)]
    vs = [vp_ref[...], vc_ref[...], vn_ref[...], cv_ref[...].astype(BF16)]
    biases = [bias_p, bias_c, bias_n, None]
    for pr in range(4):
        c = pr * LANES
        q = q_ref[:, c:c + LANES]
        o0 = _attend(_keep(q, lo), ks, vs, biases=biases, sink=sink_ref[_SW_ORDER[2 * pr]])
        o1 = _attend(_keep(q, hi), ks, vs, biases=biases, sink=sink_ref[_SW_ORDER[2 * pr + 1]])
        o_ref[:, c:c + LANES] = jnp.where(lo, o0, o1).astype(BF16)


def _attn_sw(qkv, cache_k, cache_v, sw_sink, l):
    rows = qkv.shape[0]
    nbatch = rows // 2048
    last = NT_LAT - 1

    def row(b, r):
        return b * NT_LAT + r

    def kv_specs(col):
        return [
            pl.BlockSpec((TQ, LANES), lambda b, r: (row(b, jnp.maximum(r - 1, 0)), col)),
            pl.BlockSpec((TQ, LANES), lambda b, r: (row(b, r), col)),
            pl.BlockSpec((TQ, LANES), lambda b, r: (row(b, jnp.minimum(r + 1, last)), col)),
        ]

    cache_spec = pl.BlockSpec((None, None, 512, LANES), lambda b, r: (b, l, 0, 0))
    return pl.pallas_call(
        _attn_sw_kernel,
        grid=(nbatch, NT_LAT),
        in_specs=[pl.BlockSpec((TQ, 512), lambda b, r: (row(b, r), QB // 512))]
        + kv_specs(KB // LANES) + kv_specs(VB // LANES)
        + [cache_spec, cache_spec, pl.BlockSpec(memory_space=pltpu.SMEM)],
        out_specs=pl.BlockSpec((TQ, 512), lambda b, r: (row(b, r), 0)),
        out_shape=jax.ShapeDtypeStruct((rows, 512), BF16),
        compiler_params=_cparams(("parallel", "parallel")),
        name="attn_lat_sw",
    )(qkv, qkv, qkv, qkv, qkv, qkv, qkv, cache_k, cache_v, sw_sink[l])


def _attn_da_kernel(q_ref, k_ref, v_ref, ck_ref, cv_ref, lam_ref, subg_ref, bd_ref, o_ref, *, lam_init):
    lo, quarters = _lane_masks()
    lam = _diff_lambda(lam_ref, lam_init)
    ks = [k_ref[...], ck_ref[...].astype(BF16)]
    vs = [v_ref[...], cv_ref[...].astype(BF16)]
    o = _diff_pair(q_ref[...], ks, vs, lam, subg_ref[...], bd_ref[...], lam_init, lo, quarters)
    o_ref[...] = o.astype(BF16)


def _attn_da(qkv, cache_k, cache_v, da_lambda, subg, bd128, l, lam_init):
    rows = qkv.shape[0]
    nbatch = rows // 2048
    cache_spec = pl.BlockSpec((None, None, 512, LANES), lambda b, p, r: (b, l, 0, p))
    return pl.pallas_call(
        functools.partial(_attn_da_kernel, lam_init=lam_init),
        grid=(nbatch, 2, NT_LAT),
        in_specs=[
            pl.BlockSpec((TQ, LANES), lambda b, p, r: (b * NT_LAT + r, QC // LANES + p)),
            pl.BlockSpec((2048, LANES), lambda b, p, r: (b, KC // LANES + p)),
            pl.BlockSpec((2048, LANES), lambda b, p, r: (b, VC // LANES + p)),
            cache_spec, cache_spec,
            pl.BlockSpec((None, 4, DA_DIM), lambda b, p, r: (l, 0, 0)),
            pl.BlockSpec((None, 1, LANES), lambda b, p, r: (l, 0, 0)),
            pl.BlockSpec((LANES, LANES), lambda b, p, r: (0, 0)),
        ],
        out_specs=pl.BlockSpec((TQ, LANES), lambda b, p, r: (b * NT_LAT + r, p)),
        out_shape=jax.ShapeDtypeStruct((rows, 256), BF16),
        compiler_params=_cparams(("parallel", "parallel", "parallel")),
        name="attn_lat_da",
    )(qkv, qkv, qkv, cache_k, cache_v, da_lambda, subg, bd128)


def _ffn_kernel(x_ref, xp_ref, xn_ref, oa_ref, oap_ref, oan_ref, ob_ref, obp_ref, obn_ref,
                oc_ref, ocp_ref, ocn_ref, mod_ref, g_ref, wo_ref, wg_ref, wv_ref,
                cwg_ref, cwv_ref, cbg_ref, cbv_ref, wd_ref, out_ref, x1_sc, h2_sc, acc_sc, *, seqlen):
    i = pl.program_id(0)
    j = pl.program_id(1)
    tm = x_ref.shape[0]

    @pl.when(j == 0)
    def _():
        gate1 = mod_ref[:, 2 * D_MODEL:3 * D_MODEL]
        shift2 = mod_ref[:, 3 * D_MODEL:4 * D_MODEL]
        scale2 = mod_ref[:, 4 * D_MODEL:5 * D_MODEL]

        def stage(xv, oa, ob, oc):
            mix = _dot(oa, wo_ref[0:256]) + _dot(ob, wo_ref[256:768]) + _dot(oc, wo_ref[768:1024])
            x1 = xv + gate1 * mix
            ms = jnp.mean(x1 * x1, axis=-1, keepdims=True)
            y = x1 * lax.rsqrt(ms + EPS) * g_ref[...]
            return x1, (y * (1.0 + scale2) + shift2).astype(BF16)

        x1, h2 = stage(x_ref[...], oa_ref[...], ob_ref[...], oc_ref[...])
        x1_sc[...] = x1
        h2_sc[HALO:HALO + tm] = h2
        h2_sc[0:HALO] = stage(xp_ref[...], oap_ref[...], obp_ref[...], ocp_ref[...])[1]
        h2_sc[HALO + tm:2 * HALO + tm] = stage(xn_ref[...], oan_ref[...], obn_ref[...], ocn_ref[...])[1]
        acc_sc[...] = jnp.zeros_like(acc_sc)

    h2 = h2_sc[...]
    ck = wg_ref.shape[1]
    pos = (i * tm + lax.broadcasted_iota(jnp.int32, (tm, ck), 0)) & (seqlen - 1)
    has_prev = pos != 0
    has_next = pos != seqlen - 1
    mtot = tm + 2 * HALO

    def conv(u, cw_ref, cb_ref):
        up = pltpu.roll(u, 1, 0)[HALO:HALO + tm]
        un = pltpu.roll(u, mtot - 1, 0)[HALO:HALO + tm]
        uc = u[HALO:HALO + tm]
        y = cb_ref[...] + jnp.where(has_prev, up, 0.0) * cw_ref[0:1]
        y = y + uc * cw_ref[1:2]
        return y + jnp.where(has_next, un, 0.0) * cw_ref[2:3]

    yg = conv(_dot(h2, wg_ref[...]), cwg_ref, cbg_ref)
    yv = conv(_dot(h2, wv_ref[...]), cwv_ref, cbv_ref)
    gated = (yg * _sigmoid(yg) * yv).astype(BF16)
    acc_sc[...] += _dot(gated, wd_ref[...])

    @pl.when(j == pl.num_programs(1) - 1)
    def _():
        out_ref[...] = x1_sc[...] + mod_ref[:, 5 * D_MODEL:6 * D_MODEL] * acc_sc[...]


def _ffn(x, oa, ob, oc, mod4, mod_row_fn, l, g_ffn, w_out_bf, w_up_bf, conv_w, conv_b, w_down_bf, seqlen):
    rows = x.shape[0]
    tm = TM_FFN
    ck = CK_FFN
    nj = D_FF // ck
    nhb = rows // HALO
    per = tm // HALO

    def tile(w):
        return pl.BlockSpec((tm, w), lambda i, j: (i, 0))

    def prev(w):
        return pl.BlockSpec((HALO, w), lambda i, j: (jnp.maximum(i * per - 1, 0), 0))

    def nxt(w):
        return pl.BlockSpec((HALO, w), lambda i, j: (jnp.minimum((i + 1) * per, nhb - 1), 0))

    def trio(w):
        return [tile(w), prev(w), nxt(w)]

    in_specs = trio(D_MODEL) + trio(256) + trio(512) + trio(256) + [
        pl.BlockSpec((None, None, 1, 6 * D_MODEL), lambda i, j: (l, mod_row_fn(i, tm), 0, 0)),
        pl.BlockSpec((None, 1, D_MODEL), lambda i, j: (l, 0, 0)),
        pl.BlockSpec((None, D_MIX, D_MODEL), lambda i, j: (l, 0, 0)),
        pl.BlockSpec((None, D_MODEL, ck), lambda i, j: (l, 0, j)),
        pl.BlockSpec((None, D_MODEL, ck), lambda i, j: (l, 0, nj + j)),
        pl.BlockSpec((None, 3, ck), lambda i, j: (l, 0, j)),
        pl.BlockSpec((None, 3, ck), lambda i, j: (l, 0, nj + j)),
        pl.BlockSpec((None, 1, ck), lambda i, j: (l, 0, j)),
        pl.BlockSpec((None, 1, ck), lambda i, j: (l, 0, nj + j)),
        pl.BlockSpec((None, ck, D_MODEL), lambda i, j: (l, j, 0)),
    ]
    cb3 = conv_b.reshape(DEPTH, 1, 2 * D_FF)
    return pl.pallas_call(
        functools.partial(_ffn_kernel, seqlen=seqlen),
        grid=(rows // tm, nj),
        in_specs=in_specs,
        out_specs=pl.BlockSpec((tm, D_MODEL), lambda i, j: (i, 0)),
        out_shape=jax.ShapeDtypeStruct((rows, D_MODEL), F32),
        scratch_shapes=[
            pltpu.VMEM((tm, D_MODEL), F32),
            pltpu.VMEM((tm + 2 * HALO, D_MODEL), BF16),
            pltpu.VMEM((tm, D_MODEL), F32),
        ],
        compiler_params=_cparams(("parallel", "arbitrary")),
        name="outproj_ffn",
    )(x, x, x, oa, oa, oa, ob, ob, ob, oc, oc, oc, mod4, g_ffn.reshape(DEPTH, 1, D_MODEL),
      w_out_bf, w_up_bf, w_up_bf, conv_w, conv_w, cb3, cb3, w_down_bf)


def _block_diag_mean(n, group):
    idx = np.arange(n) // group
    return jnp.asarray((idx[:, None] == idx[None, :]).astype(np.float32) / group, dtype=BF16)


def _rope_tables(T, dim):
    n = dim // 4
    inv = 1.0 / (ROPE_BASE ** (jnp.arange(n, dtype=F32) / n))
    t = jnp.arange(T)
    rowp = (t // GRID_W).astype(F32)
    colp = (t % GRID_W).astype(F32)
    ang = jnp.concatenate([rowp[:, None] * inv, colp[:, None] * inv], axis=-1)
    cos = jnp.repeat(jnp.cos(ang), 2, axis=-1)
    sin = jnp.repeat(jnp.sin(ang), 2, axis=-1) * jnp.tile(jnp.asarray([-1.0, 1.0], F32), dim // 2)
    reps = LANES // dim
    return jnp.tile(cos, (1, reps)), jnp.tile(sin, (1, reps))


def _sw_perm():
    return np.concatenate([np.arange(HEAD_DIM) + h * HEAD_DIM for h in _SW_ORDER])


def kernel(x_prompt, x_sample, cache_na_k, cache_na_v, cache_sw_k, cache_sw_v, cache_da_k, cache_da_v, c, c_ctx, g_attn, g_ffn, w_mod, b_mod, w_in, na_qk_g, na_rpb, sw_qk_g, sw_sink, da_qk_g, da_lambda, da_subln_g, w_out, w_up, conv_w, conv_b, w_down):
    nb_ctx, seq, _ = x_prompt.shape
    nb_lat, T, _ = x_sample.shape
    npast = cache_na_k.shape[2]

    perm = _sw_perm()
    in_cols = np.zeros(D_IN, np.int64)
    for mine, ref_off, w in _REF_COLS:
        in_cols[mine:mine + w] = ref_off + (perm if mine == QB else np.arange(w))
    out_rows = np.arange(D_MIX)
    out_rows[256:768] = 256 + perm
    w_in_bf = w_in[:, :, in_cols].astype(BF16)
    w_out_bf = w_out[:, out_rows, :].astype(BF16)
    w_up_bf = w_up.astype(BF16)
    w_down_bf = w_down.astype(BF16)

    ones = lambda n: jnp.ones((DEPTH, n), F32)
    sc64 = HEAD_DIM ** -0.5
    sc32 = DA_DIM ** -0.5
    gvec = jnp.concatenate([
        jnp.tile(sw_qk_g[:, 0], (1, 8)) * sc64,
        jnp.tile(na_qk_g[:, 0], (1, 4)) * sc64, jnp.tile(na_qk_g[:, 1], (1, 4)), ones(256),
        jnp.tile(sw_qk_g[:, 1], (1, 2)), ones(128),
        jnp.tile(da_qk_g[:, 0], (1, 8)) * sc32, jnp.tile(da_qk_g[:, 1], (1, 8)), ones(256),
    ], axis=-1).reshape(DEPTH, 1, D_IN)
    subg = jnp.tile(da_subln_g, (1, 2)).reshape(DEPTH, 1, LANES)
    bd64 = _block_diag_mean(256, HEAD_DIM)
    bd32 = _block_diag_mean(256, DA_DIM)
    bd128 = _block_diag_mean(LANES, HEAD_DIM)
    rope_tabs = _rope_tables(T, HEAD_DIM) + _rope_tables(T, DA_DIM)

    cna_k = cache_na_k.reshape(nb_lat, DEPTH, npast, 256)
    cna_v = cache_na_v.reshape(nb_lat, DEPTH, npast, 256)
    csw_k = cache_sw_k.reshape(nb_lat, DEPTH, npast, 128)
    csw_v = cache_sw_v.reshape(nb_lat, DEPTH, npast, 128)
    cda_k = cache_da_k.reshape(nb_lat, DEPTH, npast, 256)
    cda_v = cache_da_v.reshape(nb_lat, DEPTH, npast, 256)

    cvecs = jnp.concatenate([c_ctx[None, :], c, jnp.zeros((8 - 1 - nb_lat, D_MODEL), F32)], axis=0)
    mod4 = _modulation(cvecs, w_mod, b_mod).reshape(DEPTH, 8, 1, 6 * D_MODEL)
    bias = _na_bias_tiles(na_rpb)

    ctx_row = lambda i, tm: 0
    lat_row = lambda i, tm: 1 + (i * tm) // T

    xp = x_prompt.reshape(nb_ctx * seq, D_MODEL)
    xs = x_sample.reshape(nb_lat * T, D_MODEL)
    caches = [[] for _ in range(6)]
    for l in range(DEPTH):
        lam_init = 0.8 - 0.6 * math.exp(-0.3 * l)
        outs = _inproj(xp, mod4, ctx_row, l, g_attn, w_in_bf, gvec, bd64, bd32, None, True)
        qkv_p = outs[0]
        for lst, arr in zip(caches, outs[1:]):
            lst.append(arr)
        oa, ob, oc = _attn_ctx(qkv_p, l, sw_sink, da_lambda, subg, bd128, lam_init)
        xp = _ffn(xp, oa, ob, oc, mod4, ctx_row, l, g_ffn, w_out_bf, w_up_bf, conv_w, conv_b, w_down_bf, seq)
        qkv_s = _inproj(xs, mod4, lat_row, l, g_attn, w_in_bf, gvec, bd64, bd32, rope_tabs, False)[0]
        oa = _attn_na(qkv_s, cna_k, cna_v, bias, l)
        ob = _attn_sw(qkv_s, csw_k, csw_v, sw_sink, l)
        oc = _attn_da(qkv_s, cda_k, cda_v, da_lambda, subg, bd128, l, lam_init)
        xs = _ffn(xs, oa, ob, oc, mod4, lat_row, l, g_ffn, w_out_bf, w_up_bf, conv_w, conv_b, w_down_bf, T)

    def stack(lst, tail):
        return jnp.stack([a.reshape((nb_ctx, seq) + tail) for a in lst], axis=1)

    return (xp.reshape(nb_ctx, seq, D_MODEL), xs.reshape(nb_lat, T, D_MODEL),
            stack(caches[0], (NA_HEADS, HEAD_DIM)), stack(caches[1], (NA_HEADS, HEAD_DIM)),
            stack(caches[2], (SW_KV_HEADS, HEAD_DIM)), stack(caches[3], (SW_KV_HEADS, HEAD_DIM)),
            stack(caches[4], (DA_HEADS, 2, DA_DIM)), stack(caches[5], (DA_HEADS, DA_DIM * 2)))
```

```python
import functools
import math

import numpy as np
import jax
import jax.numpy as jnp
from jax import lax
from jax.experimental import pallas as pl
from jax.experimental.pallas import tpu as pltpu

F32 = jnp.float32
BF16 = jnp.bfloat16

D_MODEL = 1024
DEPTH = 4
GRID_W = 64
HEAD_DIM = 64
NA_HEADS = 4
NA_WIN_ROWS = 8
NA_WIN_COLS = 16
SW_HEADS = 8
SW_KV_HEADS = 2
SW_WINDOW = 128
DA_HEADS = 4
DA_DIM = 32
D_FF = 2816
ROPE_BASE = 10000.0
EPS = 1e-6

QB = 0
QA, KA, VA = 512, 768, 1024
KB, VB = 1280, 1408
QC, KC, VC = 1536, 1792, 2048
D_IN = 2304
D_MIX = 1024
_SW_ORDER = (0, 4, 1, 5, 2, 6, 3, 7)

LANES = 128
NEG = -1e30
LOG2E = math.log2(math.e)
TQ = 256
TM_IN = 512
TM_FFN = 512
HALO = 16
CK_FFN = 256
FFN_STAGE_ROWS = 256
VMEM_LIMIT = 56 * 1024 * 1024


def _cparams(sem):
    return pltpu.CompilerParams(dimension_semantics=sem, vmem_limit_bytes=VMEM_LIMIT)


def _dot(a, b):
    return jnp.dot(a, b, preferred_element_type=F32)


def _dot_nt(a, b):
    return lax.dot_general(a, b, (((1,), (1,)), ((), ())), preferred_element_type=F32)


def _sigmoid(x):
    return 1.0 / (1.0 + jnp.exp(-x))


def _mod_kernel(c_ref, w_ref, b_ref, o_ref):
    c = c_ref[...]
    s = (c * _sigmoid(c)).astype(BF16)
    o_ref[...] = _dot(s, w_ref[...].astype(BF16)) + b_ref[...]


def _modulation(cvecs, w_mod, b_mod):
    nchunk = 4
    cn = 6 * D_MODEL // nchunk
    return pl.pallas_call(
        _mod_kernel,
        grid=(DEPTH, nchunk),
        in_specs=[
            pl.BlockSpec((8, D_MODEL), lambda l, n: (0, 0)),
            pl.BlockSpec((None, D_MODEL, cn), lambda l, n: (l, 0, n)),
            pl.BlockSpec((None, 1, cn), lambda l, n: (l, 0, n)),
        ],
        out_specs=pl.BlockSpec((None, 8, cn), lambda l, n: (l, 0, n)),
        out_shape=jax.ShapeDtypeStruct((DEPTH, 8, 6 * D_MODEL), F32),
        compiler_params=_cparams(("arbitrary", "arbitrary")),
        name="modulation",
    )(cvecs, w_mod, b_mod.reshape(DEPTH, 1, 6 * D_MODEL))


N_RPB_COLS = 2 * NA_WIN_COLS - 1
_BIAS_KINDS = (
    (3, lambda i, j: j >= i),
    (7, lambda i, j: True),
    (11, lambda i, j: j < i),
    (3, lambda i, j: True),
    (11, lambda i, j: True),
    (0, lambda i, j: False),
)
K_PREV, K_CUR, K_NEXT, K_PREV_FULL, K_NEXT_FULL, K_NONE = range(6)


def _bias_kernel(rpb_ref, out_ref):
    lh = pl.program_id(0)
    qcol = lax.broadcasted_iota(jnp.int32, (GRID_W, LANES), 0)
    lane = lax.broadcasted_iota(jnp.int32, (GRID_W, LANES), 1)
    kcol = lane & (GRID_W - 1)
    hi = lane >= GRID_W
    f = jnp.clip(kcol - qcol, -(NA_WIN_COLS - 1), NA_WIN_COLS - 1) + (NA_WIN_COLS - 1)
    qs = jnp.clip(qcol - NA_WIN_COLS // 2, 0, GRID_W - NA_WIN_COLS)
    cvalid = (kcol >= qs) & (kcol < qs + NA_WIN_COLS)
    masked = jnp.full((GRID_W, LANES), NEG, F32)

    def block_pair(a0, a1):
        if a0 is None and a1 is None:
            return masked
        acc = jnp.zeros((GRID_W, LANES), F32)
        for b in range(N_RPB_COLS):
            s0 = rpb_ref[lh, a0 * N_RPB_COLS + b] if a0 is not None else 0.0
            s1 = rpb_ref[lh, a1 * N_RPB_COLS + b] if a1 is not None else 0.0
            acc = jnp.where(f == b, jnp.where(hi, s1, s0), acc)
        valid = cvalid
        if a0 is None:
            valid = valid & hi
        if a1 is None:
            valid = valid & jnp.logical_not(hi)
        return jnp.where(valid, acc * LOG2E, masked)

    for kind, (off, inside) in enumerate(_BIAS_KINDS):
        for i in range(4):
            for jt in range(2):
                a = [(j - i + off) if inside(i, j) else None for j in (2 * jt, 2 * jt + 1)]
                out_ref[kind, i * GRID_W:(i + 1) * GRID_W, jt * LANES:(jt + 1) * LANES] = block_pair(a[0], a[1])


def _na_bias_tiles(na_rpb):
    n = DEPTH * NA_HEADS
    rpb2 = na_rpb.reshape(n, (2 * NA_WIN_ROWS - 1) * N_RPB_COLS)
    out = pl.pallas_call(
        _bias_kernel,
        grid=(n,),
        in_specs=[pl.BlockSpec(memory_space=pltpu.SMEM)],
        out_specs=pl.BlockSpec((None, len(_BIAS_KINDS), TQ, TQ), lambda i: (i, 0, 0, 0)),
        out_shape=jax.ShapeDtypeStruct((n, len(_BIAS_KINDS), TQ, TQ), F32),
        compiler_params=_cparams(("arbitrary",)),
        name="na_bias_tiles",
    )(rpb2)
    return out.reshape(DEPTH, NA_HEADS, len(_BIAS_KINDS), TQ, TQ)


_NORM64 = ((QB, 256), (QB + 256, 256), (QA, 256), (KA, 256), (KB, 128))
_NORM32 = ((QC, 256), (KC, 256))
_PLAIN = ((VA, 256), (VB, 128), (VC, 256))
_ROPE_B = (QB, QB + 256, KB)
_ROPE_C = (QC, KC)
_CACHE_COLS = (KA, VA, KB, VB, KC, VC)


def _inproj_kernel(*refs, rope, n_alias):
    x_ref, mod_ref, g_ref, w_ref, gvec_ref, bd64_ref, bd32_ref = refs[:7]
    rest = refs[7:]
    if rope:
        cosb_ref, sinb_ref, cosc_ref, sinc_ref = rest[:4]
        rest = rest[4:]
    rest = rest[n_alias:]
    qkv_ref = rest[0]
    cache_refs = rest[1:]

    x = x_ref[...]
    ms = jnp.mean(x * x, axis=-1, keepdims=True)
    y = x * lax.rsqrt(ms + EPS) * g_ref[...]
    h = (y * (1.0 + mod_ref[:, D_MODEL:2 * D_MODEL]) + mod_ref[:, 0:D_MODEL]).astype(BF16)
    tm = x.shape[0]
    even = (lax.broadcasted_iota(jnp.int32, (tm, LANES), 1) & 1) == 0

    def rotary(t, cos_ref, sin_ref):
        cos = cos_ref[...]
        sin = sin_ref[...]
        parts = []
        for c in range(t.shape[1] // LANES):
            tc = t[:, c * LANES:(c + 1) * LANES]
            partner = jnp.where(even, pltpu.roll(tc, LANES - 1, 1), pltpu.roll(tc, 1, 1))
            parts.append(tc * cos + partner * sin)
        return parts[0] if len(parts) == 1 else jnp.concatenate(parts, axis=1)

    cache_of = dict(zip(_CACHE_COLS, cache_refs))

    def emit(a, w, t):
        qkv_ref[:, a:a + w] = t.astype(BF16)
        if a in cache_of:
            ref = cache_of[a]
            seq = ref.shape[1]
            for n in range(ref.shape[0]):
                ref[n] = t[n * seq:(n + 1) * seq]

    p = _dot(h, w_ref[...])
    for segs, bd_ref in ((_NORM64, bd64_ref), (_NORM32, bd32_ref)):
        for a, w in segs:
            t = p[:, a:a + w]
            msq = _dot((t * t).astype(BF16), bd_ref[0:w, 0:w])
            t = t * lax.rsqrt(msq + EPS) * gvec_ref[:, a:a + w]
            if rope and a in _ROPE_B:
                t = rotary(t, cosb_ref, sinb_ref)
            if rope and a in _ROPE_C:
                t = rotary(t, cosc_ref, sinc_ref)
            emit(a, w, t)
    for a, w in _PLAIN:
        emit(a, w, p[:, a:a + w])


def _inproj(x, mod4, mod_row_fn, l, g_attn, w_in_bf, gvec, bd64, bd32, rope_tabs, caches, seq):
    rows = x.shape[0]
    tm = TM_IN
    grid = (rows // tm,)
    rope = rope_tabs is not None
    in_specs = [
        pl.BlockSpec((tm, D_MODEL), lambda i: (i, 0)),
        pl.BlockSpec((None, None, 1, 6 * D_MODEL), lambda i: (l, mod_row_fn(i, tm), 0, 0)),
        pl.BlockSpec((None, 1, D_MODEL), lambda i: (l, 0, 0)),
        pl.BlockSpec((None, D_MODEL, D_IN), lambda i: (l, 0, 0)),
        pl.BlockSpec((None, 1, D_IN), lambda i: (l, 0, 0)),
        pl.BlockSpec((256, 256), lambda i: (0, 0)),
        pl.BlockSpec((256, 256), lambda i: (0, 0)),
    ]
    args = [x, mod4, g_attn.reshape(DEPTH, 1, D_MODEL), w_in_bf, gvec, bd64, bd32]
    if rope:
        nt = rope_tabs[0].shape[0] // tm
        in_specs += [pl.BlockSpec((tm, LANES), lambda i: (i % nt, 0))] * 4
        args += list(rope_tabs)
    out_shape = [jax.ShapeDtypeStruct((rows, D_IN), BF16)]
    out_specs = [pl.BlockSpec((tm, D_IN), lambda i: (i, 0))]
    aliases = {}
    n_alias = 0
    if caches is not None:
        per = tm // seq
        n_alias = len(caches)
        for n, w in enumerate((256, 256, 128, 128, 256, 256)):
            out_shape.append(jax.ShapeDtypeStruct((rows // seq, DEPTH, seq, w), F32))
            out_specs.append(pl.BlockSpec((per, None, seq, w), lambda i: (i, l, 0, 0)))
            if caches:
                aliases[len(args) + n] = 1 + n
        in_specs += [pl.BlockSpec(memory_space=pl.ANY)] * n_alias
        args += list(caches)
    return pl.pallas_call(
        functools.partial(_inproj_kernel, rope=rope, n_alias=n_alias),
        grid=grid,
        in_specs=in_specs,
        out_specs=out_specs,
        out_shape=out_shape,
        input_output_aliases=aliases,
        compiler_params=_cparams(("parallel",)),
        name="inproj_lat" if rope else "inproj_ctx",
    )(*args)


def _lane_masks():
    lane = lax.broadcasted_iota(jnp.int32, (1, LANES), 1)
    lo = lane < HEAD_DIM
    quarters = [(lane >= DA_DIM * j) & (lane < DA_DIM * (j + 1)) for j in range(4)]
    return lo, quarters


def _keep(q, mask):
    return jnp.where(mask, q, jnp.zeros_like(q))


def _attend(qm, ks, vs, biases=None, sink=None):
    if sink is not None:
        sink = sink * LOG2E
    ss = []
    for n, k in enumerate(ks):
        s = _dot_nt(qm, k)
        if biases is not None and biases[n] is not None:
            s = s + biases[n]
        ss.append(s)
    m = ss[0].max(axis=-1, keepdims=True)
    for s in ss[1:]:
        m = jnp.maximum(m, s.max(axis=-1, keepdims=True))
    if sink is not None:
        m = jnp.maximum(m, sink)
    l = None
    o = None
    for s, v in zip(ss, vs):
        p = jnp.exp2(s - m)
        ls = p.sum(axis=-1, keepdims=True)
        os = _dot(p.astype(BF16), v)
        l = ls if l is None else l + ls
        o = os if o is None else o + os
    if sink is not None:
        l = l + jnp.exp2(sink - m)
    return o * (1.0 / l)


def _diff_lambda(lam_ref, lam_init):
    lp = lam_ref[...]
    a = jnp.sum(lp[0:1] * lp[1:2], axis=-1, keepdims=True)
    b = jnp.sum(lp[2:3] * lp[3:4], axis=-1, keepdims=True)
    return jnp.exp(a) - jnp.exp(b) + lam_init


def _diff_pair(q, ks, vs, lam, subg, bd, lam_init, lo, quarters):
    o = [_attend(_keep(q, quarters[j]), ks, vs) for j in range(4)]
    d = jnp.where(lo, o[0] - lam * o[1], o[2] - lam * o[3])
    msq = _dot((d * d).astype(BF16), bd)
    return d * lax.rsqrt(msq + EPS) * subg * (1.0 - lam_init)


def _attn_ctx_kernel(qkv_ref, sink_ref, lam_ref, subg_ref, bd_ref, oa_ref, ob_ref, oc_ref, *, lam_init):
    lo, quarters = _lane_masks()
    hi = jnp.logical_not(lo)
    for pr in range(2):
        c = pr * LANES
        q = qkv_ref[:, QA + c:QA + c + LANES]
        k = [qkv_ref[:, KA + c:KA + c + LANES]]
        v = [qkv_ref[:, VA + c:VA + c + LANES]]
        o = jnp.where(lo, _attend(_keep(q, lo), k, v), _attend(_keep(q, hi), k, v))
        oa_ref[:, c:c + LANES] = o.astype(BF16)
    k = [qkv_ref[:, KB:KB + LANES]]
    v = [qkv_ref[:, VB:VB + LANES]]
    for pr in range(4):
        c = pr * LANES
        q = qkv_ref[:, QB + c:QB + c + LANES]
        o0 = _attend(_keep(q, lo), k, v, sink=sink_ref[_SW_ORDER[2 * pr]])
        o1 = _attend(_keep(q, hi), k, v, sink=sink_ref[_SW_ORDER[2 * pr + 1]])
        ob_ref[:, c:c + LANES] = jnp.where(lo, o0, o1).astype(BF16)
    lam = _diff_lambda(lam_ref, lam_init)
    for pr in range(2):
        c = pr * LANES
        q = qkv_ref[:, QC + c:QC + c + LANES]
        k = [qkv_ref[:, KC + c:KC + c + LANES]]
        v = [qkv_ref[:, VC + c:VC + c + LANES]]
        o = _diff_pair(q, k, v, lam, subg_ref[...], bd_ref[...], lam_init, lo, quarters)
        oc_ref[:, c:c + LANES] = o.astype(BF16)


def _attn_ctx(qkv, l, sw_sink, da_lambda, subg, bd128, lam_init):
    rows = qkv.shape[0]
    nb = rows // TQ
    return pl.pallas_call(
        functools.partial(_attn_ctx_kernel, lam_init=lam_init),
        grid=(nb,),
        in_specs=[
            pl.BlockSpec((TQ, D_IN), lambda b: (b, 0)),
            pl.BlockSpec(memory_space=pltpu.SMEM),
            pl.BlockSpec((None, 4, DA_DIM), lambda b: (l, 0, 0)),
            pl.BlockSpec((None, 1, LANES), lambda b: (l, 0, 0)),
            pl.BlockSpec((LANES, LANES), lambda b: (0, 0)),
        ],
        out_specs=[
            pl.BlockSpec((TQ, 256), lambda b: (b, 0)),
            pl.BlockSpec((TQ, 512), lambda b: (b, 0)),
            pl.BlockSpec((TQ, 256), lambda b: (b, 0)),
        ],
        out_shape=[
            jax.ShapeDtypeStruct((rows, 256), BF16),
            jax.ShapeDtypeStruct((rows, 512), BF16),
            jax.ShapeDtypeStruct((rows, 256), BF16),
        ],
        compiler_params=_cparams(("parallel",)),
        name="attn_ctx",
    )(qkv, sw_sink[l], da_lambda, subg, bd128)


NT_LAT = 2048 // TQ


def _attn_na_kernel(q_ref, kp_ref, kc_ref, kn_ref, vp_ref, vc_ref, vn_ref, ck_ref, cv_ref,
                    bp_ref, bc_ref, bn_ref, o_ref):
    lo, _ = _lane_masks()
    hi = jnp.logical_not(lo)
    ck = ck_ref[...].astype(BF16)
    cv = cv_ref[...].astype(BF16)
    for pr in range(2):
        c = pr * LANES
        sl = slice(c, c + LANES)
        q = q_ref[:, sl]
        ks = [kp_ref[:, sl], kc_ref[:, sl], kn_ref[:, sl], ck[:, sl]]
        vs = [vp_ref[:, sl], vc_ref[:, sl], vn_ref[:, sl], cv[:, sl]]
        outs = []
        for hl, mask in enumerate((lo, hi)):
            h = 2 * pr + hl
            outs.append(_attend(_keep(q, mask), ks, vs, biases=[bp_ref[h], bc_ref[h], bn_ref[h], None]))
        o_ref[:, sl] = jnp.where(lo, outs[0], outs[1]).astype(BF16)


def _attn_na(qkv, cache_k, cache_v, bias, l):
    rows = qkv.shape[0]
    nbatch = rows // 2048
    last = NT_LAT - 1

    def row(b, r):
        return b * NT_LAT + r

    def kv_specs(col):
        return [
            pl.BlockSpec((TQ, 256), lambda r, b: (row(b, jnp.maximum(r - 1, 0)), col)),
            pl.BlockSpec((TQ, 256), lambda r, b: (row(b, r), col)),
            pl.BlockSpec((TQ, 256), lambda r, b: (row(b, jnp.minimum(r + 1, last)), col)),
        ]

    def kind_prev(r):
        return jnp.where(r == 0, K_NONE, jnp.where(r == last, K_PREV_FULL, K_PREV))

    def kind_next(r):
        return jnp.where(r == 0, K_NEXT_FULL, jnp.where(r == last, K_NONE, K_NEXT))

    def bias_spec(kind_fn):
        return pl.BlockSpec((None, NA_HEADS, None, TQ, TQ), lambda r, b: (l, 0, kind_fn(r), 0, 0))

    cache_spec = pl.BlockSpec((None, None, 512, 256), lambda r, b: (b, l, 0, 0))
    return pl.pallas_call(
        _attn_na_kernel,
        grid=(NT_LAT, nbatch),
        in_specs=[pl.BlockSpec((TQ, 256), lambda r, b: (row(b, r), QA // 256))]
        + kv_specs(KA // 256) + kv_specs(VA // 256)
        + [cache_spec, cache_spec,
           bias_spec(kind_prev), bias_spec(lambda r: K_CUR), bias_spec(kind_next)],
        out_specs=pl.BlockSpec((TQ, 256), lambda r, b: (row(b, r), 0)),
        out_shape=jax.ShapeDtypeStruct((rows, 256), BF16),
        compiler_params=_cparams(("parallel", "parallel")),
        name="attn_lat_na",
    )(qkv, qkv, qkv, qkv, qkv, qkv, qkv, cache_k, cache_v, bias, bias, bias)


def _attn_sw_kernel(q_ref, k0_ref, k1_ref, k2_ref, k3_ref, v0_ref, v1_ref, v2_ref, v3_ref,
                    ck_ref, cv_ref, sink_ref, o_ref):
    r = pl.program_id(1)
    lo, _ = _lane_masks()
    hi = jnp.logical_not(lo)
    qi = lax.broadcasted_iota(jnp.int32, (TQ, TQ), 0)
    kj = lax.broadcasted_iota(jnp.int32, (TQ, TQ), 1)
    zero = jnp.zeros((TQ, TQ), F32)
    neg = jnp.full((TQ, TQ), NEG, F32)
    bias_a = jnp.where((kj >= qi) & ((r > 0) | (kj >= SW_WINDOW)), zero, neg)
    bias_b = jnp.where((kj <= qi) & ((r < NT_LAT - 1) | (kj < SW_WINDOW)), zero, neg)
    ks = [jnp.concatenate([k0_ref[...], k1_ref[...]], axis=0),
          jnp.concatenate([k2_ref[...], k3_ref[...]], axis=0), ck_ref[...].astype(BF16)]
    vs = [jnp.concatenate([v0_ref[...], v1_ref[...]], axis=0),
          jnp.concatenate([v2_ref[...], v3_ref[...]], axis=0), cv_ref[...].astype(BF16)]
    biases = [bias_a, bias_b, None]
    for pr in range(4):
        c = pr * LANES
        q = q_ref[:, c:c + LANES]
        o0 = _attend(_keep(q, lo), ks, vs, biases=biases, sink=sink_ref[_SW_ORDER[2 * pr]])
        o1 = _attend(_keep(q, hi), ks, vs, biases=biases, sink=sink_ref[_SW_ORDER[2 * pr + 1]])
        o_ref[:, c:c + LANES] = jnp.where(lo, o0, o1).astype(BF16)


def _attn_sw(qkv, cache_k, cache_v, sw_sink, l):
    rows = qkv.shape[0]
    nbatch = rows // 2048
    nblk = 2048 // SW_WINDOW

    def kv_specs(col):
        return [pl.BlockSpec((SW_WINDOW, LANES),
                             lambda b, r, t=t: (b * nblk + jnp.clip(2 * r - 1 + t, 0, nblk - 1), col))
                for t in range(4)]

    cache_spec = pl.BlockSpec((None, None, 512, LANES), lambda b, r: (b, l, 0, 0))
    return pl.pallas_call(
        _attn_sw_kernel,
        grid=(nbatch, NT_LAT),
        in_specs=[pl.BlockSpec((TQ, 512), lambda b, r: (b * NT_LAT + r, QB // 512))]
        + kv_specs(KB // LANES) + kv_specs(VB // LANES)
        + [cache_spec, cache_spec, pl.BlockSpec(memory_space=pltpu.SMEM)],
        out_specs=pl.BlockSpec((TQ, 512), lambda b, r: (b * NT_LAT + r, 0)),
        out_shape=jax.ShapeDtypeStruct((rows, 512), BF16),
        compiler_params=_cparams(("parallel", "parallel")),
        name="attn_lat_sw",
    )(*([qkv] * 9), cache_k, cache_v, sw_sink[l])


def _attn_da_kernel(q_ref, k_ref, v_ref, ck_ref, cv_ref, lam_ref, subg_ref, bd_ref, o_ref, *, lam_init):
    lo, quarters = _lane_masks()
    lam = _diff_lambda(lam_ref, lam_init)
    ks = [k_ref[...], ck_ref[...].astype(BF16)]
    vs = [v_ref[...], cv_ref[...].astype(BF16)]
    o = _diff_pair(q_ref[...], ks, vs, lam, subg_ref[...], bd_ref[...], lam_init, lo, quarters)
    o_ref[...] = o.astype(BF16)


def _attn_da(qkv, cache_k, cache_v, da_lambda, subg, bd128, l, lam_init):
    rows = qkv.shape[0]
    nbatch = rows // 2048
    cache_spec = pl.BlockSpec((None, None, 512, LANES), lambda b, p, r: (b, l, 0, p))
    return pl.pallas_call(
        functools.partial(_attn_da_kernel, lam_init=lam_init),
        grid=(nbatch, 2, NT_LAT),
        in_specs=[
            pl.BlockSpec((TQ, LANES), lambda b, p, r: (b * NT_LAT + r, QC // LANES + p)),
            pl.BlockSpec((2048, LANES), lambda b, p, r: (b, KC // LANES + p)),
            pl.BlockSpec((2048, LANES), lambda b, p, r: (b, VC // LANES + p)),
            cache_spec, cache_spec,
            pl.BlockSpec((None, 4, DA_DIM), lambda b, p, r: (l, 0, 0)),
            pl.BlockSpec((None, 1, LANES), lambda b, p, r: (l, 0, 0)),
            pl.BlockSpec((LANES, LANES), lambda b, p, r: (0, 0)),
        ],
        out_specs=pl.BlockSpec((TQ, LANES), lambda b, p, r: (b * NT_LAT + r, p)),
        out_shape=jax.ShapeDtypeStruct((rows, 256), BF16),
        compiler_params=_cparams(("parallel", "parallel", "parallel")),
        name="attn_lat_da",
    )(qkv, qkv, qkv, cache_k, cache_v, da_lambda, subg, bd128)


def _ffn_kernel(x_ref, xp_ref, xn_ref, oa_ref, oap_ref, oan_ref, ob_ref, obp_ref, obn_ref,
                oc_ref, ocp_ref, ocn_ref, mod_ref, g_ref, wo_ref, wup_ref, cw_ref, cb_ref, wd_ref,
                out_ref, x1_sc, h2_sc, acc_sc, ug0_sc, uv0_sc, ug1_sc, uv1_sc, *, seqlen):
    i = pl.program_id(0)
    tm = x_ref.shape[0]

    def prepare():
        gate1 = mod_ref[:, 2 * D_MODEL:3 * D_MODEL]
        shift2 = mod_ref[:, 3 * D_MODEL:4 * D_MODEL]
        scale2 = mod_ref[:, 4 * D_MODEL:5 * D_MODEL]

        def stage(xv, oa, ob, oc):
            mix = _dot(oa, wo_ref[0:256]) + _dot(ob, wo_ref[256:768]) + _dot(oc, wo_ref[768:1024])
            x1 = xv + gate1 * mix
            ms = jnp.mean(x1 * x1, axis=-1, keepdims=True)
            y = x1 * lax.rsqrt(ms + EPS) * g_ref[...]
            return x1, (y * (1.0 + scale2) + shift2).astype(BF16)

        for r0 in range(0, tm, FFN_STAGE_ROWS):
            r1 = r0 + FFN_STAGE_ROWS
            x1, h2 = stage(x_ref[r0:r1], oa_ref[r0:r1], ob_ref[r0:r1], oc_ref[r0:r1])
            x1_sc[r0:r1] = x1
            h2_sc[HALO + r0:HALO + r1] = h2
        h2_sc[0:HALO] = stage(xp_ref[...], oap_ref[...], obp_ref[...], ocp_ref[...])[1]
        h2_sc[HALO + tm:2 * HALO + tm] = stage(xn_ref[...], oan_ref[...], obn_ref[...], ocn_ref[...])[1]

    prepare()
    h2 = h2_sc[...]
    ck = CK_FFN
    pos = (i * tm + lax.broadcasted_iota(jnp.int32, (tm, ck), 0)) & (seqlen - 1)
    has_prev = pos != 0
    has_next = pos != seqlen - 1
    mtot = tm + 2 * HALO

    def conv(u, col):
        cw = cw_ref[:, col:col + ck]
        up = pltpu.roll(u, 1, 0)[HALO:HALO + tm]
        un = pltpu.roll(u, mtot - 1, 0)[HALO:HALO + tm]
        y = cb_ref[:, col:col + ck] + jnp.where(has_prev, up, 0.0) * cw[0:1]
        y = y + u[HALO:HALO + tm] * cw[1:2]
        return y + jnp.where(has_next, un, 0.0) * cw[2:3]

    nj = D_FF // ck
    bufs = ((ug0_sc, uv0_sc), (ug1_sc, uv1_sc))

    def up_project(c):
        ug_sc, uv_sc = bufs[c % 2]
        ug_sc[...] = _dot(h2, wup_ref[:, c * ck:(c + 1) * ck])
        uv_sc[...] = _dot(h2, wup_ref[:, D_FF + c * ck:D_FF + (c + 1) * ck])

    def down_project(c):
        ug_sc, uv_sc = bufs[c % 2]
        yg = conv(ug_sc[...], c * ck)
        yv = conv(uv_sc[...], D_FF + c * ck)
        gated = (yg * _sigmoid(yg) * yv).astype(BF16)
        part = _dot(gated, wd_ref[c * ck:(c + 1) * ck])
        if c == 0:
            acc_sc[...] = part
        else:
            acc_sc[...] += part

    up_project(0)
    for c in range(nj):
        if c + 1 < nj:
            up_project(c + 1)
        down_project(c)
    out_ref[...] = x1_sc[...] + mod_ref[:, 5 * D_MODEL:6 * D_MODEL] * acc_sc[...]


def _ffn(x, oa, ob, oc, mod4, mod_row_fn, l, g_ffn, w_out_bf, w_up_bf, conv_w, conv_b, w_down_bf, seqlen):
    rows = x.shape[0]
    tm = TM_FFN
    nhb = rows // HALO
    per = tm // HALO

    def tile(w):
        return pl.BlockSpec((tm, w), lambda i: (i, 0))

    def prev(w):
        return pl.BlockSpec((HALO, w), lambda i: (jnp.maximum(i * per - 1, 0), 0))

    def nxt(w):
        return pl.BlockSpec((HALO, w), lambda i: (jnp.minimum((i + 1) * per, nhb - 1), 0))

    def trio(w):
        return [tile(w), prev(w), nxt(w)]

    def resident(shape):
        return pl.BlockSpec((None,) + shape, lambda i: (l, 0, 0), pipeline_mode=pl.Buffered(1))

    in_specs = trio(D_MODEL) + trio(256) + trio(512) + trio(256) + [
        pl.BlockSpec((None, None, 1, 6 * D_MODEL), lambda i: (l, mod_row_fn(i, tm), 0, 0)),
        resident((1, D_MODEL)),
        resident((D_MIX, D_MODEL)),
        resident((D_MODEL, 2 * D_FF)),
        resident((3, 2 * D_FF)),
        resident((1, 2 * D_FF)),
        resident((D_FF, D_MODEL)),
    ]
    return pl.pallas_call(
        functools.partial(_ffn_kernel, seqlen=seqlen),
        grid=(rows // tm,),
        in_specs=in_specs,
        out_specs=pl.BlockSpec((tm, D_MODEL), lambda i: (i, 0)),
        out_shape=jax.ShapeDtypeStruct((rows, D_MODEL), F32),
        scratch_shapes=[
            pltpu.VMEM((tm, D_MODEL), F32),
            pltpu.VMEM((tm + 2 * HALO, D_MODEL), BF16),
            pltpu.VMEM((tm, D_MODEL), F32),
        ] + [pltpu.VMEM((tm + 2 * HALO, CK_FFN), F32)] * 4,
        compiler_params=_cparams(("parallel",)),
        name="outproj_ffn",
    )(x, x, x, oa, oa, oa, ob, ob, ob, oc, oc, oc, mod4, g_ffn.reshape(DEPTH, 1, D_MODEL),
      w_out_bf, w_up_bf, conv_w, conv_b.reshape(DEPTH, 1, 2 * D_FF), w_down_bf)


def _block_diag_mean(n, group):
    idx = np.arange(n) // group
    return jnp.asarray((idx[:, None] == idx[None, :]).astype(np.float32) / group, dtype=BF16)


def _rope_tables(T, dim):
    n = dim // 4
    inv = 1.0 / (ROPE_BASE ** (jnp.arange(n, dtype=F32) / n))
    t = jnp.arange(T)
    rowp = (t // GRID_W).astype(F32)
    colp = (t % GRID_W).astype(F32)
    ang = jnp.concatenate([rowp[:, None] * inv, colp[:, None] * inv], axis=-1)
    cos = jnp.repeat(jnp.cos(ang), 2, axis=-1)
    sin = jnp.repeat(jnp.sin(ang), 2, axis=-1) * jnp.tile(jnp.asarray([-1.0, 1.0], F32), dim // 2)
    reps = LANES // dim
    return jnp.tile(cos, (1, reps)), jnp.tile(sin, (1, reps))


def _pair_sw_heads(w, axis):
    assert _SW_ORDER == tuple(g * 4 + i for i in range(4) for g in range(2))
    shp = w.shape
    w = w.reshape(shp[:axis] + (2, 4, HEAD_DIM) + shp[axis + 1:])
    return jnp.swapaxes(w, axis, axis + 1).reshape(shp)


def kernel(x_prompt, x_sample, cache_na_k, cache_na_v, cache_sw_k, cache_sw_v, cache_da_k, cache_da_v, c, c_ctx, g_attn, g_ffn, w_mod, b_mod, w_in, na_qk_g, na_rpb, sw_qk_g, sw_sink, da_qk_g, da_lambda, da_subln_g, w_out, w_up, conv_w, conv_b, w_down):
    nb_ctx, seq, _ = x_prompt.shape
    nb_lat, T, _ = x_sample.shape
    npast = cache_na_k.shape[2]

    w_in_bf = jnp.concatenate([
        _pair_sw_heads(w_in[:, :, 768:1280], 2), w_in[:, :, 0:768], w_in[:, :, 1280:]], axis=2).astype(BF16)
    w_out_bf = jnp.concatenate([
        w_out[:, 0:256], _pair_sw_heads(w_out[:, 256:768], 1), w_out[:, 768:]], axis=1).astype(BF16)
    w_up_bf = w_up.astype(BF16)
    w_down_bf = w_down.astype(BF16)

    ones = lambda n: jnp.ones((DEPTH, n), F32)
    sc64 = HEAD_DIM ** -0.5 * LOG2E
    sc32 = DA_DIM ** -0.5 * LOG2E
    gvec = jnp.concatenate([
        jnp.tile(sw_qk_g[:, 0], (1, 8)) * sc64,
        jnp.tile(na_qk_g[:, 0], (1, 4)) * sc64, jnp.tile(na_qk_g[:, 1], (1, 4)), ones(256),
        jnp.tile(sw_qk_g[:, 1], (1, 2)), ones(128),
        jnp.tile(da_qk_g[:, 0], (1, 8)) * sc32, jnp.tile(da_qk_g[:, 1], (1, 8)), ones(256),
    ], axis=-1).reshape(DEPTH, 1, D_IN)
    subg = jnp.tile(da_subln_g, (1, 2)).reshape(DEPTH, 1, LANES)
    bd64 = _block_diag_mean(256, HEAD_DIM)
    bd32 = _block_diag_mean(256, DA_DIM)
    bd128 = _block_diag_mean(LANES, HEAD_DIM)
    rope_tabs = _rope_tables(T, HEAD_DIM) + _rope_tables(T, DA_DIM)

    cna_k = cache_na_k.reshape(nb_lat, DEPTH, npast, 256)
    cna_v = cache_na_v.reshape(nb_lat, DEPTH, npast, 256)
    csw_k = cache_sw_k.reshape(nb_lat, DEPTH, npast, 128)
    csw_v = cache_sw_v.reshape(nb_lat, DEPTH, npast, 128)
    cda_k = cache_da_k.reshape(nb_lat, DEPTH, npast, 256)
    cda_v = cache_da_v.reshape(nb_lat, DEPTH, npast, 256)

    cvecs = jnp.concatenate([c_ctx[None, :], c, jnp.zeros((8 - 1 - nb_lat, D_MODEL), F32)], axis=0)
    mod4 = _modulation(cvecs, w_mod, b_mod).reshape(DEPTH, 8, 1, 6 * D_MODEL)
    bias = _na_bias_tiles(na_rpb)

    ctx_row = lambda i, tm: 0
    lat_row = lambda i, tm: 1 + (i * tm) // T

    xp = x_prompt.reshape(nb_ctx * seq, D_MODEL)
    xs = x_sample.reshape(nb_lat * T, D_MODEL)
    caches = ()
    for l in range(DEPTH):
        lam_init = 0.8 - 0.6 * math.exp(-0.3 * l)
        qkv_p, *caches = _inproj(xp, mod4, ctx_row, l, g_attn, w_in_bf, gvec, bd64, bd32, None, caches, seq)
        oa, ob, oc = _attn_ctx(qkv_p, l, sw_sink, da_lambda, subg, bd128, lam_init)
        xp = _ffn(xp, oa, ob, oc, mod4, ctx_row, l, g_ffn, w_out_bf, w_up_bf, conv_w, conv_b, w_down_bf, seq)
        qkv_s = _inproj(xs, mod4, lat_row, l, g_attn, w_in_bf, gvec, bd64, bd32, rope_tabs, None, T)[0]
        oa = _attn_na(qkv_s, cna_k, cna_v, bias, l)
        ob = _attn_sw(qkv_s, csw_k, csw_v, sw_sink, l)
        oc = _attn_da(qkv_s, cda_k, cda_v, da_lambda, subg, bd128, l, lam_init)
        xs = _ffn(xs, oa, ob, oc, mod4, lat_row, l, g_ffn, w_out_bf, w_up_bf, conv_w, conv_b, w_down_bf, T)

    def heads(a, tail):
        return a.reshape((nb_ctx, DEPTH, seq) + tail)

    return (xp.reshape(nb_ctx, seq, D_MODEL), xs.reshape(nb_lat, T, D_MODEL),
            heads(caches[0], (NA_HEADS, HEAD_DIM)), heads(caches[1], (NA_HEADS, HEAD_DIM)),
            heads(caches[2], (SW_KV_HEADS, HEAD_DIM)), heads(caches[3], (SW_KV_HEADS, HEAD_DIM)),
            heads(caches[4], (DA_HEADS, 2, DA_DIM)), heads(caches[5], (DA_HEADS, DA_DIM * 2)))
```

```python
import functools
import math

import numpy as np
import jax
import jax.numpy as jnp
from jax import lax
from jax.experimental import pallas as pl
from jax.experimental.pallas import tpu as pltpu

F32 = jnp.float32
BF16 = jnp.bfloat16

D_MODEL = 1024
DEPTH = 4
GRID_W = 64
HEAD_DIM = 64
NA_HEADS = 4
NA_WIN_ROWS = 8
NA_WIN_COLS = 16
SW_HEADS = 8
SW_KV_HEADS = 2
SW_WINDOW = 128
DA_HEADS = 4
DA_DIM = 32
D_FF = 2816
ROPE_BASE = 10000.0
EPS = 1e-6

QB = 0
QA, KA, VA = 512, 768, 1024
KB, VB = 1280, 1408
QC, KC, VC = 1536, 1792, 2048
D_IN = 2304
D_MIX = 1024
_SW_ORDER = (0, 4, 1, 5, 2, 6, 3, 7)

LANES = 128
NEG = -1e30
LOG2E = math.log2(math.e)
TQ = 256
TM_IN = 512
TM_FFN = 512
HALO = 16
CK_FFN = 256
FFN_STAGE_ROWS = 256
VMEM_LIMIT = 56 * 1024 * 1024


def _cparams(sem):
    return pltpu.CompilerParams(dimension_semantics=sem, vmem_limit_bytes=VMEM_LIMIT)


def _dot(a, b):
    return jnp.dot(a, b, preferred_element_type=F32)


def _dot_nt(a, b):
    return lax.dot_general(a, b, (((1,), (1,)), ((), ())), preferred_element_type=F32)


def _sigmoid(x):
    return 1.0 / (1.0 + jnp.exp(-x))


def _mod_kernel(c_ref, w_ref, b_ref, o_ref):
    c = c_ref[...]
    s = (c * _sigmoid(c)).astype(BF16)
    o_ref[...] = _dot(s, w_ref[...].astype(BF16)) + b_ref[...]


def _modulation(cvecs, w_mod, b_mod):
    nchunk = 4
    cn = 6 * D_MODEL // nchunk
    return pl.pallas_call(
        _mod_kernel,
        grid=(DEPTH, nchunk),
        in_specs=[
            pl.BlockSpec((8, D_MODEL), lambda l, n: (0, 0)),
            pl.BlockSpec((None, D_MODEL, cn), lambda l, n: (l, 0, n)),
            pl.BlockSpec((None, 1, cn), lambda l, n: (l, 0, n)),
        ],
        out_specs=pl.BlockSpec((None, 8, cn), lambda l, n: (l, 0, n)),
        out_shape=jax.ShapeDtypeStruct((DEPTH, 8, 6 * D_MODEL), F32),
        compiler_params=_cparams(("arbitrary", "arbitrary")),
        name="modulation",
    )(cvecs, w_mod, b_mod.reshape(DEPTH, 1, 6 * D_MODEL))


N_RPB_COLS = 2 * NA_WIN_COLS - 1
_BIAS_KINDS = (
    (3, lambda i, j: j >= i),
    (7, lambda i, j: True),
    (11, lambda i, j: j < i),
    (3, lambda i, j: True),
    (11, lambda i, j: True),
    (0, lambda i, j: False),
)
K_PREV, K_CUR, K_NEXT, K_PREV_FULL, K_NEXT_FULL, K_NONE = range(6)


def _bias_kernel(rpb_ref, out_ref):
    lh = pl.program_id(0)
    qcol = lax.broadcasted_iota(jnp.int32, (GRID_W, LANES), 0)
    lane = lax.broadcasted_iota(jnp.int32, (GRID_W, LANES), 1)
    kcol = lane & (GRID_W - 1)
    hi = lane >= GRID_W
    f = jnp.clip(kcol - qcol, -(NA_WIN_COLS - 1), NA_WIN_COLS - 1) + (NA_WIN_COLS - 1)
    qs = jnp.clip(qcol - NA_WIN_COLS // 2, 0, GRID_W - NA_WIN_COLS)
    cvalid = (kcol >= qs) & (kcol < qs + NA_WIN_COLS)
    masked = jnp.full((GRID_W, LANES), NEG, F32)

    def block_pair(a0, a1):
        if a0 is None and a1 is None:
            return masked
        acc = jnp.zeros((GRID_W, LANES), F32)
        for b in range(N_RPB_COLS):
            s0 = rpb_ref[lh, a0 * N_RPB_COLS + b] if a0 is not None else 0.0
            s1 = rpb_ref[lh, a1 * N_RPB_COLS + b] if a1 is not None else 0.0
            acc = jnp.where(f == b, jnp.where(hi, s1, s0), acc)
        valid = cvalid
        if a0 is None:
            valid = valid & hi
        if a1 is None:
            valid = valid & jnp.logical_not(hi)
        return jnp.where(valid, acc * LOG2E, masked)

    for kind, (off, inside) in enumerate(_BIAS_KINDS):
        for i in range(4):
            for jt in range(2):
                a = [(j - i + off) if inside(i, j) else None for j in (2 * jt, 2 * jt + 1)]
                out_ref[kind, i * GRID_W:(i + 1) * GRID_W, jt * LANES:(jt + 1) * LANES] = block_pair(a[0], a[1])


def _na_bias_tiles(na_rpb):
    n = DEPTH * NA_HEADS
    rpb2 = na_rpb.reshape(n, (2 * NA_WIN_ROWS - 1) * N_RPB_COLS)
    out = pl.pallas_call(
        _bias_kernel,
        grid=(n,),
        in_specs=[pl.BlockSpec(memory_space=pltpu.SMEM)],
        out_specs=pl.BlockSpec((None, len(_BIAS_KINDS), TQ, TQ), lambda i: (i, 0, 0, 0)),
        out_shape=jax.ShapeDtypeStruct((n, len(_BIAS_KINDS), TQ, TQ), F32),
        compiler_params=_cparams(("arbitrary",)),
        name="na_bias_tiles",
    )(rpb2)
    return out.reshape(DEPTH, NA_HEADS, len(_BIAS_KINDS), TQ, TQ)


_NORM64 = ((QB, 256), (QB + 256, 256), (QA, 256), (KA, 256), (KB, 128))
_NORM32 = ((QC, 256), (KC, 256))
_PLAIN = ((VA, 256), (VB, 128), (VC, 256))
_ROPE_B = (QB, QB + 256, KB)
_ROPE_C = (QC, KC)
_CACHE_COLS = (KA, VA, KB, VB, KC, VC)


def _inproj_kernel(*refs, rope, n_alias):
    x_ref, mod_ref, g_ref, w_ref, gvec_ref, bd64_ref, bd32_ref = refs[:7]
    rest = refs[7:]
    if rope:
        cosb_ref, sinb_ref, cosc_ref, sinc_ref = rest[:4]
        rest = rest[4:]
    rest = rest[n_alias:]
    qkv_ref = rest[0]
    cache_refs = rest[1:]

    x = x_ref[...]
    ms = jnp.mean(x * x, axis=-1, keepdims=True)
    y = x * lax.rsqrt(ms + EPS) * g_ref[...]
    h = (y * (1.0 + mod_ref[:, D_MODEL:2 * D_MODEL]) + mod_ref[:, 0:D_MODEL]).astype(BF16)
    tm = x.shape[0]
    even = (lax.broadcasted_iota(jnp.int32, (tm, LANES), 1) & 1) == 0

    def rotary(t, cos_ref, sin_ref):
        cos = cos_ref[...]
        sin = sin_ref[...]
        parts = []
        for c in range(t.shape[1] // LANES):
            tc = t[:, c * LANES:(c + 1) * LANES]
            partner = jnp.where(even, pltpu.roll(tc, LANES - 1, 1), pltpu.roll(tc, 1, 1))
            parts.append(tc * cos + partner * sin)
        return parts[0] if len(parts) == 1 else jnp.concatenate(parts, axis=1)

    cache_of = dict(zip(_CACHE_COLS, cache_refs))

    def emit(a, w, t):
        qkv_ref[:, a:a + w] = t.astype(BF16)
        if a in cache_of:
            ref = cache_of[a]
            seq = ref.shape[1]
            for n in range(ref.shape[0]):
                ref[n] = t[n * seq:(n + 1) * seq]

    p = _dot(h, w_ref[...])
    for segs, bd_ref in ((_NORM64, bd64_ref), (_NORM32, bd32_ref)):
        for a, w in segs:
            t = p[:, a:a + w]
            msq = _dot((t * t).astype(BF16), bd_ref[0:w, 0:w])
            t = t * lax.rsqrt(msq + EPS) * gvec_ref[:, a:a + w]
            if rope and a in _ROPE_B:
                t = rotary(t, cosb_ref, sinb_ref)
            if rope and a in _ROPE_C:
                t = rotary(t, cosc_ref, sinc_ref)
            emit(a, w, t)
    for a, w in _PLAIN:
        emit(a, w, p[:, a:a + w])


def _inproj(x, mod4, mod_row_fn, l, g_attn, w_in_bf, gvec, bd64, bd32, rope_tabs, caches, seq):
    rows = x.shape[0]
    tm = TM_IN
    grid = (rows // tm,)
    rope = rope_tabs is not None
    in_specs = [
        pl.BlockSpec((tm, D_MODEL), lambda i: (i, 0)),
        pl.BlockSpec((None, None, 1, 6 * D_MODEL), lambda i: (l, mod_row_fn(i, tm), 0, 0)),
        pl.BlockSpec((None, 1, D_MODEL), lambda i: (l, 0, 0)),
        pl.BlockSpec((None, D_MODEL, D_IN), lambda i: (l, 0, 0)),
        pl.BlockSpec((None, 1, D_IN), lambda i: (l, 0, 0)),
        pl.BlockSpec((256, 256), lambda i: (0, 0)),
        pl.BlockSpec((256, 256), lambda i: (0, 0)),
    ]
    args = [x, mod4, g_attn.reshape(DEPTH, 1, D_MODEL), w_in_bf, gvec, bd64, bd32]
    if rope:
        nt = rope_tabs[0].shape[0] // tm
        in_specs += [pl.BlockSpec((tm, LANES), lambda i: (i % nt, 0))] * 4
        args += list(rope_tabs)
    out_shape = [jax.ShapeDtypeStruct((rows, D_IN), BF16)]
    out_specs = [pl.BlockSpec((tm, D_IN), lambda i: (i, 0))]
    aliases = {}
    n_alias = 0
    if caches is not None:
        per = tm // seq
        n_alias = len(caches)
        for n, w in enumerate((256, 256, 128, 128, 256, 256)):
            out_shape.append(jax.ShapeDtypeStruct((rows // seq, DEPTH, seq, w), F32))
            out_specs.append(pl.BlockSpec((per, None, seq, w), lambda i: (i, l, 0, 0)))
            if caches:
                aliases[len(args) + n] = 1 + n
        in_specs += [pl.BlockSpec(memory_space=pl.ANY)] * n_alias
        args += list(caches)
    return pl.pallas_call(
        functools.partial(_inproj_kernel, rope=rope, n_alias=n_alias),
        grid=grid,
        in_specs=in_specs,
        out_specs=out_specs,
        out_shape=out_shape,
        input_output_aliases=aliases,
        compiler_params=_cparams(("parallel",)),
        name="inproj_lat" if rope else "inproj_ctx",
    )(*args)


def _lane_masks():
    lane = lax.broadcasted_iota(jnp.int32, (1, LANES), 1)
    lo = lane < HEAD_DIM
    quarters = [(lane >= DA_DIM * j) & (lane < DA_DIM * (j + 1)) for j in range(4)]
    return lo, quarters


def _keep(q, mask):
    return jnp.where(mask, q, jnp.zeros_like(q))


def _attend(qm, ks, vs, biases=None, sink=None):
    if sink is not None:
        sink = sink * LOG2E
    ss = []
    for n, k in enumerate(ks):
        s = _dot_nt(qm, k)
        if biases is not None and biases[n] is not None:
            s = s + biases[n]
        ss.append(s)
    m = ss[0].max(axis=-1, keepdims=True)
    for s in ss[1:]:
        m = jnp.maximum(m, s.max(axis=-1, keepdims=True))
    if sink is not None:
        m = jnp.maximum(m, sink)
    l = None
    o = None
    for s, v in zip(ss, vs):
        p = jnp.exp2(s - m)
        ls = p.sum(axis=-1, keepdims=True)
        os = _dot(p.astype(BF16), v)
        l = ls if l is None else l + ls
        o = os if o is None else o + os
    if sink is not None:
        l = l + jnp.exp2(sink - m)
    return o * (1.0 / l)


def _diff_lambda(lam_ref, lam_init):
    lp = lam_ref[...]
    a = jnp.sum(lp[0:1] * lp[1:2], axis=-1, keepdims=True)
    b = jnp.sum(lp[2:3] * lp[3:4], axis=-1, keepdims=True)
    return jnp.exp(a) - jnp.exp(b) + lam_init


def _diff_pair(q, ks, vs, lam, subg, bd, lam_init, lo, quarters):
    o = [_attend(_keep(q, quarters[j]), ks, vs) for j in range(4)]
    d = jnp.where(lo, o[0] - lam * o[1], o[2] - lam * o[3])
    msq = _dot((d * d).astype(BF16), bd)
    return d * lax.rsqrt(msq + EPS) * subg * (1.0 - lam_init)


def _attn_ctx_kernel(qkv_ref, sink_ref, lam_ref, subg_ref, bd_ref, oa_ref, ob_ref, oc_ref, *, lam_init):
    lo, quarters = _lane_masks()
    hi = jnp.logical_not(lo)
    for pr in range(2):
        c = pr * LANES
        q = qkv_ref[:, QA + c:QA + c + LANES]
        k = [qkv_ref[:, KA + c:KA + c + LANES]]
        v = [qkv_ref[:, VA + c:VA + c + LANES]]
        o = jnp.where(lo, _attend(_keep(q, lo), k, v), _attend(_keep(q, hi), k, v))
        oa_ref[:, c:c + LANES] = o.astype(BF16)
    k = [qkv_ref[:, KB:KB + LANES]]
    v = [qkv_ref[:, VB:VB + LANES]]
    for pr in range(4):
        c = pr * LANES
        q = qkv_ref[:, QB + c:QB + c + LANES]
        o0 = _attend(_keep(q, lo), k, v, sink=sink_ref[_SW_ORDER[2 * pr]])
        o1 = _attend(_keep(q, hi), k, v, sink=sink_ref[_SW_ORDER[2 * pr + 1]])
        ob_ref[:, c:c + LANES] = jnp.where(lo, o0, o1).astype(BF16)
    lam = _diff_lambda(lam_ref, lam_init)
    for pr in range(2):
        c = pr * LANES
        q = qkv_ref[:, QC + c:QC + c + LANES]
        k = [qkv_ref[:, KC + c:KC + c + LANES]]
        v = [qkv_ref[:, VC + c:VC + c + LANES]]
        o = _diff_pair(q, k, v, lam, subg_ref[...], bd_ref[...], lam_init, lo, quarters)
        oc_ref[:, c:c + LANES] = o.astype(BF16)


def _attn_ctx(qkv, l, sw_sink, da_lambda, subg, bd128, lam_init):
    rows = qkv.shape[0]
    nb = rows // TQ
    return pl.pallas_call(
        functools.partial(_attn_ctx_kernel, lam_init=lam_init),
        grid=(nb,),
        in_specs=[
            pl.BlockSpec((TQ, D_IN), lambda b: (b, 0)),
            pl.BlockSpec(memory_space=pltpu.SMEM),
            pl.BlockSpec((None, 4, DA_DIM), lambda b: (l, 0, 0)),
            pl.BlockSpec((None, 1, LANES), lambda b: (l, 0, 0)),
            pl.BlockSpec((LANES, LANES), lambda b: (0, 0)),
        ],
        out_specs=[
            pl.BlockSpec((TQ, 256), lambda b: (b, 0)),
            pl.BlockSpec((TQ, 512), lambda b: (b, 0)),
            pl.BlockSpec((TQ, 256), lambda b: (b, 0)),
        ],
        out_shape=[
            jax.ShapeDtypeStruct((rows, 256), BF16),
            jax.ShapeDtypeStruct((rows, 512), BF16),
            jax.ShapeDtypeStruct((rows, 256), BF16),
        ],
        compiler_params=_cparams(("parallel",)),
        name="attn_ctx",
    )(qkv, sw_sink[l], da_lambda, subg, bd128)


NT_LAT = 2048 // TQ


def _attn_na_kernel(q_ref, kp_ref, kc_ref, kn_ref, vp_ref, vc_ref, vn_ref, ck_ref, cv_ref,
                    bp_ref, bc_ref, bn_ref, o_ref):
    lo, _ = _lane_masks()
    hi = jnp.logical_not(lo)
    ck = ck_ref[...].astype(BF16)
    cv = cv_ref[...].astype(BF16)
    for pr in range(2):
        c = pr * LANES
        sl = slice(c, c + LANES)
        q = q_ref[:, sl]
        ks = [kp_ref[:, sl], kc_ref[:, sl], kn_ref[:, sl], ck[:, sl]]
        vs = [vp_ref[:, sl], vc_ref[:, sl], vn_ref[:, sl], cv[:, sl]]
        outs = []
        for hl, mask in enumerate((lo, hi)):
            h = 2 * pr + hl
            outs.append(_attend(_keep(q, mask), ks, vs, biases=[bp_ref[h], bc_ref[h], bn_ref[h], None]))
        o_ref[:, sl] = jnp.where(lo, outs[0], outs[1]).astype(BF16)


def _attn_na(qkv, cache_k, cache_v, bias, l):
    rows = qkv.shape[0]
    nbatch = rows // 2048
    last = NT_LAT - 1

    def row(b, r):
        return b * NT_LAT + r

    def kv_specs(col):
        return [
            pl.BlockSpec((TQ, 256), lambda r, b: (row(b, jnp.maximum(r - 1, 0)), col)),
            pl.BlockSpec((TQ, 256), lambda r, b: (row(b, r), col)),
            pl.BlockSpec((TQ, 256), lambda r, b: (row(b, jnp.minimum(r + 1, last)), col)),
        ]

    def kind_prev(r):
        return jnp.where(r == 0, K_NONE, jnp.where(r == last, K_PREV_FULL, K_PREV))

    def kind_next(r):
        return jnp.where(r == 0, K_NEXT_FULL, jnp.where(r == last, K_NONE, K_NEXT))

    def bias_spec(kind_fn):
        return pl.BlockSpec((None, NA_HEADS, None, TQ, TQ), lambda r, b: (l, 0, kind_fn(r), 0, 0))

    cache_spec = pl.BlockSpec((None, None, 512, 256), lambda r, b: (b, l, 0, 0))
    return pl.pallas_call(
        _attn_na_kernel,
        grid=(NT_LAT, nbatch),
        in_specs=[pl.BlockSpec((TQ, 256), lambda r, b: (row(b, r), QA // 256))]
        + kv_specs(KA // 256) + kv_specs(VA // 256)
        + [cache_spec, cache_spec,
           bias_spec(kind_prev), bias_spec(lambda r: K_CUR), bias_spec(kind_next)],
        out_specs=pl.BlockSpec((TQ, 256), lambda r, b: (row(b, r), 0)),
        out_shape=jax.ShapeDtypeStruct((rows, 256), BF16),
        compiler_params=_cparams(("parallel", "parallel")),
        name="attn_lat_na",
    )(qkv, qkv, qkv, qkv, qkv, qkv, qkv, cache_k, cache_v, bias, bias, bias)


def _attn_sw_kernel(q_ref, k0_ref, k1_ref, k2_ref, k3_ref, v0_ref, v1_ref, v2_ref, v3_ref,
                    ck_ref, cv_ref, sink_ref, o_ref):
    r = pl.program_id(1)
    lo, _ = _lane_masks()
    hi = jnp.logical_not(lo)
    qi = lax.broadcasted_iota(jnp.int32, (TQ, TQ), 0)
    kj = lax.broadcasted_iota(jnp.int32, (TQ, TQ), 1)
    zero = jnp.zeros((TQ, TQ), F32)
    neg = jnp.full((TQ, TQ), NEG, F32)
    bias_a = jnp.where((kj >= qi) & ((r > 0) | (kj >= SW_WINDOW)), zero, neg)
    bias_b = jnp.where((kj <= qi) & ((r < NT_LAT - 1) | (kj < SW_WINDOW)), zero, neg)
    ks = [jnp.concatenate([k0_ref[...], k1_ref[...]], axis=0),
          jnp.concatenate([k2_ref[...], k3_ref[...]], axis=0), ck_ref[...].astype(BF16)]
    vs = [jnp.concatenate([v0_ref[...], v1_ref[...]], axis=0),
          jnp.concatenate([v2_ref[...], v3_ref[...]], axis=0), cv_ref[...].astype(BF16)]
    biases = [bias_a, bias_b, None]
    for pr in range(4):
        c = pr * LANES
        q = q_ref[:, c:c + LANES]
        o0 = _attend(_keep(q, lo), ks, vs, biases=biases, sink=sink_ref[_SW_ORDER[2 * pr]])
        o1 = _attend(_keep(q, hi), ks, vs, biases=biases, sink=sink_ref[_SW_ORDER[2 * pr + 1]])
        o_ref[:, c:c + LANES] = jnp.where(lo, o0, o1).astype(BF16)


def _attn_sw(qkv, cache_k, cache_v, sw_sink, l):
    rows = qkv.shape[0]
    nbatch = rows // 2048
    nblk = 2048 // SW_WINDOW

    def kv_specs(col):
        return [pl.BlockSpec((SW_WINDOW, LANES),
                             lambda b, r, t=t: (b * nblk + jnp.clip(2 * r - 1 + t, 0, nblk - 1), col))
                for t in range(4)]

    cache_spec = pl.BlockSpec((None, None, 512, LANES), lambda b, r: (b, l, 0, 0))
    return pl.pallas_call(
        _attn_sw_kernel,
        grid=(nbatch, NT_LAT),
        in_specs=[pl.BlockSpec((TQ, 512), lambda b, r: (b * NT_LAT + r, QB // 512))]
        + kv_specs(KB // LANES) + kv_specs(VB // LANES)
        + [cache_spec, cache_spec, pl.BlockSpec(memory_space=pltpu.SMEM)],
        out_specs=pl.BlockSpec((TQ, 512), lambda b, r: (b * NT_LAT + r, 0)),
        out_shape=jax.ShapeDtypeStruct((rows, 512), BF16),
        compiler_params=_cparams(("parallel", "parallel")),
        name="attn_lat_sw",
    )(*([qkv] * 9), cache_k, cache_v, sw_sink[l])


def _attn_da_kernel(q_ref, k_ref, v_ref, ck_ref, cv_ref, lam_ref, subg_ref, bd_ref, o_ref, *, lam_init):
    lo, quarters = _lane_masks()
    lam = _diff_lambda(lam_ref, lam_init)
    ks = [k_ref[...], ck_ref[...].astype(BF16)]
    vs = [v_ref[...], cv_ref[...].astype(BF16)]
    o = _diff_pair(q_ref[...], ks, vs, lam, subg_ref[...], bd_ref[...], lam_init, lo, quarters)
    o_ref[...] = o.astype(BF16)


def _attn_da(qkv, cache_k, cache_v, da_lambda, subg, bd128, l, lam_init):
    rows = qkv.shape[0]
    nbatch = rows // 2048
    cache_spec = pl.BlockSpec((None, None, 512, LANES), lambda b, p, r: (b, l, 0, p))
    return pl.pallas_call(
        functools.partial(_attn_da_kernel, lam_init=lam_init),
        grid=(nbatch, 2, NT_LAT),
        in_specs=[
            pl.BlockSpec((TQ, LANES), lambda b, p, r: (b * NT_LAT + r, QC // LANES + p)),
            pl.BlockSpec((2048, LANES), lambda b, p, r: (b, KC // LANES + p)),
            pl.BlockSpec((2048, LANES), lambda b, p, r: (b, VC // LANES + p)),
            cache_spec, cache_spec,
            pl.BlockSpec((None, 4, DA_DIM), lambda b, p, r: (l, 0, 0)),
            pl.BlockSpec((None, 1, LANES), lambda b, p, r: (l, 0, 0)),
            pl.BlockSpec((LANES, LANES), lambda b, p, r: (0, 0)),
        ],
        out_specs=pl.BlockSpec((TQ, LANES), lambda b, p, r: (b * NT_LAT + r, p)),
        out_shape=jax.ShapeDtypeStruct((rows, 256), BF16),
        compiler_params=_cparams(("parallel", "parallel", "parallel")),
        name="attn_lat_da",
    )(qkv, qkv, qkv, cache_k, cache_v, da_lambda, subg, bd128)


def _ffn_kernel(x_ref, xp_ref, xn_ref, oa_ref, oap_ref, oan_ref, ob_ref, obp_ref, obn_ref,
                oc_ref, ocp_ref, ocn_ref, mod_ref, g_ref, wo_ref, wup_ref, cw_ref, cb_ref, wd_ref,
                out_ref, x1_sc, h2_sc, gated_sc, ug0_sc, uv0_sc, ug1_sc, uv1_sc, *, seqlen):
    i = pl.program_id(0)
    tm = x_ref.shape[0]

    def prepare():
        gate1 = mod_ref[:, 2 * D_MODEL:3 * D_MODEL]
        shift2 = mod_ref[:, 3 * D_MODEL:4 * D_MODEL]
        scale2 = mod_ref[:, 4 * D_MODEL:5 * D_MODEL]

        def stage(xv, oa, ob, oc):
            mix = _dot(oa, wo_ref[0:256]) + _dot(ob, wo_ref[256:768]) + _dot(oc, wo_ref[768:1024])
            x1 = xv + gate1 * mix
            ms = jnp.mean(x1 * x1, axis=-1, keepdims=True)
            y = x1 * lax.rsqrt(ms + EPS) * g_ref[...]
            return x1, (y * (1.0 + scale2) + shift2).astype(BF16)

        for r0 in range(0, tm, FFN_STAGE_ROWS):
            r1 = r0 + FFN_STAGE_ROWS
            x1, h2 = stage(x_ref[r0:r1], oa_ref[r0:r1], ob_ref[r0:r1], oc_ref[r0:r1])
            x1_sc[r0:r1] = x1
            h2_sc[HALO + r0:HALO + r1] = h2
        h2_sc[0:HALO] = stage(xp_ref[...], oap_ref[...], obp_ref[...], ocp_ref[...])[1]
        h2_sc[HALO + tm:2 * HALO + tm] = stage(xn_ref[...], oan_ref[...], obn_ref[...], ocn_ref[...])[1]

    prepare()
    h2 = h2_sc[...]
    ck = CK_FFN
    mtot = tm + 2 * HALO
    sub = 8
    span = min(seqlen, tm)
    row8 = lax.broadcasted_iota(jnp.int32, (sub, ck), 0)

    def zero_rows(v, group_starts, edge):
        pieces = []
        cur = 0
        for g0 in group_starts:
            if g0 > cur:
                pieces.append(v[cur:g0])
            pos = (i * tm + g0 + row8) & (seqlen - 1)
            pieces.append(jnp.where(pos != edge, v[g0:g0 + sub], 0.0))
            cur = g0 + sub
        if cur < tm:
            pieces.append(v[cur:])
        return jnp.concatenate(pieces, axis=0)

    first_groups = list(range(0, tm, span))
    last_groups = [g + span - sub for g in first_groups]

    def conv(u, col):
        cw = cw_ref[:, col:col + ck]
        up = zero_rows(pltpu.roll(u, 1, 0)[HALO:HALO + tm], first_groups, 0)
        un = zero_rows(pltpu.roll(u, mtot - 1, 0)[HALO:HALO + tm], last_groups, seqlen - 1)
        y = cb_ref[:, col:col + ck] + up * cw[0:1]
        y = y + u[HALO:HALO + tm] * cw[1:2]
        return y + un * cw[2:3]

    def silu_mul(g, v):
        h = 0.5 * g
        return (h * jnp.tanh(h) + h) * v

    nj = D_FF // ck
    bufs = ((ug0_sc, uv0_sc), (ug1_sc, uv1_sc))

    def up_project(c):
        ug_sc, uv_sc = bufs[c % 2]
        ug_sc[...] = _dot(h2, wup_ref[:, c * ck:(c + 1) * ck])
        uv_sc[...] = _dot(h2, wup_ref[:, D_FF + c * ck:D_FF + (c + 1) * ck])

    def gate(c):
        ug_sc, uv_sc = bufs[c % 2]
        yg = conv(ug_sc[...], c * ck)
        yv = conv(uv_sc[...], D_FF + c * ck)
        gated_sc[:, c * ck:(c + 1) * ck] = silu_mul(yg, yv).astype(BF16)

    up_project(0)
    for c in range(nj):
        if c + 1 < nj:
            up_project(c + 1)
        gate(c)
    ffn = _dot(gated_sc[...], wd_ref[...])
    out_ref[...] = x1_sc[...] + mod_ref[:, 5 * D_MODEL:6 * D_MODEL] * ffn


def _ffn(x, oa, ob, oc, mod4, mod_row_fn, l, g_ffn, w_out_bf, w_up_bf, conv_w, conv_b, w_down_bf, seqlen):
    rows = x.shape[0]
    tm = TM_FFN
    nhb = rows // HALO
    per = tm // HALO

    def tile(w):
        return pl.BlockSpec((tm, w), lambda i: (i, 0))

    def prev(w):
        return pl.BlockSpec((HALO, w), lambda i: (jnp.maximum(i * per - 1, 0), 0))

    def nxt(w):
        return pl.BlockSpec((HALO, w), lambda i: (jnp.minimum((i + 1) * per, nhb - 1), 0))

    def trio(w):
        return [tile(w), prev(w), nxt(w)]

    def resident(shape):
        return pl.BlockSpec((None,) + shape, lambda i: (l, 0, 0), pipeline_mode=pl.Buffered(1))

    in_specs = trio(D_MODEL) + trio(256) + trio(512) + trio(256) + [
        pl.BlockSpec((None, None, 1, 6 * D_MODEL), lambda i: (l, mod_row_fn(i, tm), 0, 0)),
        resident((1, D_MODEL)),
        resident((D_MIX, D_MODEL)),
        resident((D_MODEL, 2 * D_FF)),
        resident((3, 2 * D_FF)),
        resident((1, 2 * D_FF)),
        resident((D_FF, D_MODEL)),
    ]
    return pl.pallas_call(
        functools.partial(_ffn_kernel, seqlen=seqlen),
        grid=(rows // tm,),
        in_specs=in_specs,
        out_specs=pl.BlockSpec((tm, D_MODEL), lambda i: (i, 0)),
        out_shape=jax.ShapeDtypeStruct((rows, D_MODEL), F32),
        scratch_shapes=[
            pltpu.VMEM((tm, D_MODEL), F32),
            pltpu.VMEM((tm + 2 * HALO, D_MODEL), BF16),
            pltpu.VMEM((tm, D_FF), BF16),
        ] + [pltpu.VMEM((tm + 2 * HALO, CK_FFN), F32)] * 4,
        compiler_params=_cparams(("parallel",)),
        name="outproj_ffn",
    )(x, x, x, oa, oa, oa, ob, ob, ob, oc, oc, oc, mod4, g_ffn.reshape(DEPTH, 1, D_MODEL),
      w_out_bf, w_up_bf, conv_w, conv_b.reshape(DEPTH, 1, 2 * D_FF), w_down_bf)


def _block_diag_mean(n, group):
    idx = np.arange(n) // group
    return jnp.asarray((idx[:, None] == idx[None, :]).astype(np.float32) / group, dtype=BF16)


def _rope_tables(T, dim):
    n = dim // 4
    inv = 1.0 / (ROPE_BASE ** (jnp.arange(n, dtype=F32) / n))
    t = jnp.arange(T)
    rowp = (t // GRID_W).astype(F32)
    colp = (t % GRID_W).astype(F32)
    ang = jnp.concatenate([rowp[:, None] * inv, colp[:, None] * inv], axis=-1)
    cos = jnp.repeat(jnp.cos(ang), 2, axis=-1)
    sin = jnp.repeat(jnp.sin(ang), 2, axis=-1) * jnp.tile(jnp.asarray([-1.0, 1.0], F32), dim // 2)
    reps = LANES // dim
    return jnp.tile(cos, (1, reps)), jnp.tile(sin, (1, reps))


def _pair_sw_heads(w, axis):
    assert _SW_ORDER == tuple(g * 4 + i for i in range(4) for g in range(2))
    shp = w.shape
    w = w.reshape(shp[:axis] + (2, 4, HEAD_DIM) + shp[axis + 1:])
    return jnp.swapaxes(w, axis, axis + 1).reshape(shp)


def kernel(x_prompt, x_sample, cache_na_k, cache_na_v, cache_sw_k, cache_sw_v, cache_da_k, cache_da_v, c, c_ctx, g_attn, g_ffn, w_mod, b_mod, w_in, na_qk_g, na_rpb, sw_qk_g, sw_sink, da_qk_g, da_lambda, da_subln_g, w_out, w_up, conv_w, conv_b, w_down):
    nb_ctx, seq, _ = x_prompt.shape
    nb_lat, T, _ = x_sample.shape
    npast = cache_na_k.shape[2]

    w_in_bf = jnp.concatenate([
        _pair_sw_heads(w_in[:, :, 768:1280], 2), w_in[:, :, 0:768], w_in[:, :, 1280:]], axis=2).astype(BF16)
    w_out_bf = jnp.concatenate([
        w_out[:, 0:256], _pair_sw_heads(w_out[:, 256:768], 1), w_out[:, 768:]], axis=1).astype(BF16)
    w_up_bf = w_up.astype(BF16)
    w_down_bf = w_down.astype(BF16)

    ones = lambda n: jnp.ones((DEPTH, n), F32)
    sc64 = HEAD_DIM ** -0.5 * LOG2E
    sc32 = DA_DIM ** -0.5 * LOG2E
    gvec = jnp.concatenate([
        jnp.tile(sw_qk_g[:, 0], (1, 8)) * sc64,
        jnp.tile(na_qk_g[:, 0], (1, 4)) * sc64, jnp.tile(na_qk_g[:, 1], (1, 4)), ones(256),
        jnp.tile(sw_qk_g[:, 1], (1, 2)), ones(128),
        jnp.tile(da_qk_g[:, 0], (1, 8)) * sc32, jnp.tile(da_qk_g[:, 1], (1, 8)), ones(256),
    ], axis=-1).reshape(DEPTH, 1, D_IN)
    subg = jnp.tile(da_subln_g, (1, 2)).reshape(DEPTH, 1, LANES)
    bd64 = _block_diag_mean(256, HEAD_DIM)
    bd32 = _block_diag_mean(256, DA_DIM)
    bd128 = _block_diag_mean(LANES, HEAD_DIM)
    rope_tabs = _rope_tables(T, HEAD_DIM) + _rope_tables(T, DA_DIM)

    cna_k = cache_na_k.reshape(nb_lat, DEPTH, npast, 256)
    cna_v = cache_na_v.reshape(nb_lat, DEPTH, npast, 256)
    csw_k = cache_sw_k.reshape(nb_lat, DEPTH, npast, 128)
    csw_v = cache_sw_v.reshape(nb_lat, DEPTH, npast, 128)
    cda_k = cache_da_k.reshape(nb_lat, DEPTH, npast, 256)
    cda_v = cache_da_v.reshape(nb_lat, DEPTH, npast, 256)

    cvecs = jnp.concatenate([c_ctx[None, :], c, jnp.zeros((8 - 1 - nb_lat, D_MODEL), F32)], axis=0)
    mod4 = _modulation(cvecs, w_mod, b_mod).reshape(DEPTH, 8, 1, 6 * D_MODEL)
    bias = _na_bias_tiles(na_rpb)

    ctx_row = lambda i, tm: 0
    lat_row = lambda i, tm: 1 + (i * tm) // T

    xp = x_prompt.reshape(nb_ctx * seq, D_MODEL)
    xs = x_sample.reshape(nb_lat * T, D_MODEL)
    caches = ()
    for l in range(DEPTH):
        lam_init = 0.8 - 0.6 * math.exp(-0.3 * l)
        qkv_p, *caches = _inproj(xp, mod4, ctx_row, l, g_attn, w_in_bf, gvec, bd64, bd32, None, caches, seq)
        oa, ob, oc = _attn_ctx(qkv_p, l, sw_sink, da_lambda, subg, bd128, lam_init)
        xp = _ffn(xp, oa, ob, oc, mod4, ctx_row, l, g_ffn, w_out_bf, w_up_bf, conv_w, conv_b, w_down_bf, seq)
        qkv_s = _inproj(xs, mod4, lat_row, l, g_attn, w_in_bf, gvec, bd64, bd32, rope_tabs, None, T)[0]
        oa = _attn_na(qkv_s, cna_k, cna_v, bias, l)
        ob = _attn_sw(qkv_s, csw_k, csw_v, sw_sink, l)
        oc = _attn_da(qkv_s, cda_k, cda_v, da_lambda, subg, bd128, l, lam_init)
        xs = _ffn(xs, oa, ob, oc, mod4, lat_row, l, g_ffn, w_out_bf, w_up_bf, conv_w, conv_b, w_down_bf, T)

    def heads(a, tail):
        return a.reshape((nb_ctx, DEPTH, seq) + tail)

    return (xp.reshape(nb_ctx, seq, D_MODEL), xs.reshape(nb_lat, T, D_MODEL),
            heads(caches[0], (NA_HEADS, HEAD_DIM)), heads(caches[1], (NA_HEADS, HEAD_DIM)),
            heads(caches[2], (SW_KV_HEADS, HEAD_DIM)), heads(caches[3], (SW_KV_HEADS, HEAD_DIM)),
            heads(caches[4], (DA_HEADS, 2, DA_DIM)), heads(caches[5], (DA_HEADS, DA_DIM * 2)))
```

```python
import functools
import math

import numpy as np
import jax
import jax.numpy as jnp
from jax import lax
from jax.experimental import pallas as pl
from jax.experimental.pallas import tpu as pltpu

F32 = jnp.float32
BF16 = jnp.bfloat16

D_MODEL = 1024
DEPTH = 4
GRID_W = 64
HEAD_DIM = 64
NA_HEADS = 4
NA_WIN_ROWS = 8
NA_WIN_COLS = 16
SW_HEADS = 8
SW_KV_HEADS = 2
SW_WINDOW = 128
DA_HEADS = 4
DA_DIM = 32
D_FF = 2816
ROPE_BASE = 10000.0
EPS = 1e-6

QB = 0
QA, KA, VA = 512, 768, 1024
KB, VB = 1280, 1408
QC, KC, VC = 1536, 1792, 2048
D_IN = 2304
D_MIX = 1024
_SW_ORDER = (0, 4, 1, 5, 2, 6, 3, 7)

LANES = 128
NEG = -1e30
LOG2E = math.log2(math.e)
TQ = 256
TQ_DA = 512
TM_IN = 512
TM_FFN = 512
HALO = 16
CK_FFN = 256
FFN_STAGE_ROWS = 256
VMEM_LIMIT = 56 * 1024 * 1024


def _cparams(sem):
    return pltpu.CompilerParams(dimension_semantics=sem, vmem_limit_bytes=VMEM_LIMIT)


def _dot(a, b):
    return jnp.dot(a, b, preferred_element_type=F32)


def _dot_nt(a, b):
    return lax.dot_general(a, b, (((1,), (1,)), ((), ())), preferred_element_type=F32)


def _sigmoid(x):
    return 1.0 / (1.0 + jnp.exp(-x))


def _mod_kernel(c_ref, w_ref, b_ref, o_ref):
    c = c_ref[...]
    s = (c * _sigmoid(c)).astype(BF16)
    o_ref[...] = _dot(s, w_ref[...].astype(BF16)) + b_ref[...]


def _modulation(cvecs, w_mod, b_mod):
    nchunk = 4
    cn = 6 * D_MODEL // nchunk
    return pl.pallas_call(
        _mod_kernel,
        grid=(DEPTH, nchunk),
        in_specs=[
            pl.BlockSpec((8, D_MODEL), lambda l, n: (0, 0)),
            pl.BlockSpec((None, D_MODEL, cn), lambda l, n: (l, 0, n)),
            pl.BlockSpec((None, 1, cn), lambda l, n: (l, 0, n)),
        ],
        out_specs=pl.BlockSpec((None, 8, cn), lambda l, n: (l, 0, n)),
        out_shape=jax.ShapeDtypeStruct((DEPTH, 8, 6 * D_MODEL), F32),
        compiler_params=_cparams(("arbitrary", "arbitrary")),
        name="modulation",
    )(cvecs, w_mod, b_mod.reshape(DEPTH, 1, 6 * D_MODEL))


N_RPB_COLS = 2 * NA_WIN_COLS - 1
_BIAS_KINDS = (
    (3, lambda i, j: j >= i),
    (7, lambda i, j: True),
    (11, lambda i, j: j < i),
    (3, lambda i, j: True),
    (11, lambda i, j: True),
    (0, lambda i, j: False),
)
K_PREV, K_CUR, K_NEXT, K_PREV_FULL, K_NEXT_FULL, K_NONE = range(6)


def _bias_kernel(rpb_ref, out_ref):
    lh = pl.program_id(0)
    qcol = lax.broadcasted_iota(jnp.int32, (GRID_W, LANES), 0)
    lane = lax.broadcasted_iota(jnp.int32, (GRID_W, LANES), 1)
    kcol = lane & (GRID_W - 1)
    hi = lane >= GRID_W
    f = jnp.clip(kcol - qcol, -(NA_WIN_COLS - 1), NA_WIN_COLS - 1) + (NA_WIN_COLS - 1)
    qs = jnp.clip(qcol - NA_WIN_COLS // 2, 0, GRID_W - NA_WIN_COLS)
    cvalid = (kcol >= qs) & (kcol < qs + NA_WIN_COLS)
    masked = jnp.full((GRID_W, LANES), NEG, F32)

    @functools.lru_cache(maxsize=None)
    def block_pair(a0, a1):
        if a0 is None and a1 is None:
            return masked
        acc = jnp.zeros((GRID_W, LANES), F32)
        for b in range(N_RPB_COLS):
            s0 = rpb_ref[lh, a0 * N_RPB_COLS + b] if a0 is not None else 0.0
            s1 = rpb_ref[lh, a1 * N_RPB_COLS + b] if a1 is not None else 0.0
            acc = jnp.where(f == b, jnp.where(hi, s1, s0), acc)
        valid = cvalid
        if a0 is None:
            valid = valid & hi
        if a1 is None:
            valid = valid & jnp.logical_not(hi)
        return jnp.where(valid, acc * LOG2E, masked)

    for kind, (off, inside) in enumerate(_BIAS_KINDS):
        for i in range(4):
            for jt in range(2):
                a = [(j - i + off) if inside(i, j) else None for j in (2 * jt, 2 * jt + 1)]
                out_ref[kind, i * GRID_W:(i + 1) * GRID_W, jt * LANES:(jt + 1) * LANES] = block_pair(a[0], a[1])


def _na_bias_tiles(na_rpb):
    n = DEPTH * NA_HEADS
    rpb2 = na_rpb.reshape(n, (2 * NA_WIN_ROWS - 1) * N_RPB_COLS)
    out = pl.pallas_call(
        _bias_kernel,
        grid=(n,),
        in_specs=[pl.BlockSpec(memory_space=pltpu.SMEM)],
        out_specs=pl.BlockSpec((None, len(_BIAS_KINDS), TQ, TQ), lambda i: (i, 0, 0, 0)),
        out_shape=jax.ShapeDtypeStruct((n, len(_BIAS_KINDS), TQ, TQ), F32),
        compiler_params=_cparams(("arbitrary",)),
        name="na_bias_tiles",
    )(rpb2)
    return out.reshape(DEPTH, NA_HEADS, len(_BIAS_KINDS), TQ, TQ)


_NORM64 = ((QB, 256), (QB + 256, 256), (QA, 256), (KA, 256), (KB, 128))
_NORM32 = ((QC, 256), (KC, 256))
_PLAIN = ((VA, 256), (VB, 128), (VC, 256))
_ROPE_B = (QB, QB + 256, KB)
_ROPE_C = (QC, KC)
_CACHE_COLS = (KA, VA, KB, VB, KC, VC)


def _inproj_kernel(*refs, rope, n_alias):
    x_ref, mod_ref, g_ref, w_ref, gvec_ref, bd64_ref, bd32_ref = refs[:7]
    rest = refs[7:]
    if rope:
        cosb_ref, sinb_ref, cosc_ref, sinc_ref = rest[:4]
        rest = rest[4:]
    rest = rest[n_alias:]
    qkv_ref = rest[0]
    cache_refs = rest[1:]

    x = x_ref[...]
    ms = jnp.mean(x * x, axis=-1, keepdims=True)
    y = x * lax.rsqrt(ms + EPS) * g_ref[...]
    h = (y * (1.0 + mod_ref[:, D_MODEL:2 * D_MODEL]) + mod_ref[:, 0:D_MODEL]).astype(BF16)
    tm = x.shape[0]
    even = (lax.broadcasted_iota(jnp.int32, (tm, LANES), 1) & 1) == 0

    def rotary(t, cos_ref, sin_ref):
        cos = cos_ref[...]
        sin = sin_ref[...]
        parts = []
        for c in range(t.shape[1] // LANES):
            tc = t[:, c * LANES:(c + 1) * LANES]
            partner = jnp.where(even, pltpu.roll(tc, LANES - 1, 1), pltpu.roll(tc, 1, 1))
            parts.append(tc * cos + partner * sin)
        return parts[0] if len(parts) == 1 else jnp.concatenate(parts, axis=1)

    cache_of = dict(zip(_CACHE_COLS, cache_refs))

    def emit(a, w, t):
        qkv_ref[:, a:a + w] = t.astype(BF16)
        if a in cache_of:
            ref = cache_of[a]
            seq = ref.shape[1]
            for n in range(ref.shape[0]):
                ref[n] = t[n * seq:(n + 1) * seq]

    p = _dot(h, w_ref[...])
    for segs, bd_ref in ((_NORM64, bd64_ref), (_NORM32, bd32_ref)):
        for a, w in segs:
            t = p[:, a:a + w]
            msq = _dot((t * t).astype(BF16), bd_ref[0:w, 0:w])
            t = t * lax.rsqrt(msq + EPS) * gvec_ref[:, a:a + w]
            if rope and a in _ROPE_B:
                t = rotary(t, cosb_ref, sinb_ref)
            if rope and a in _ROPE_C:
                t = rotary(t, cosc_ref, sinc_ref)
            emit(a, w, t)
    for a, w in _PLAIN:
        emit(a, w, p[:, a:a + w])


def _inproj(x, mod4, mod_row_fn, l, g_attn, w_in_bf, gvec, bd64, bd32, rope_tabs, caches, seq):
    rows = x.shape[0]
    tm = TM_IN
    grid = (rows // tm,)
    rope = rope_tabs is not None
    in_specs = [
        pl.BlockSpec((tm, D_MODEL), lambda i: (i, 0)),
        pl.BlockSpec((None, None, 1, 6 * D_MODEL), lambda i: (l, mod_row_fn(i, tm), 0, 0)),
        pl.BlockSpec((None, 1, D_MODEL), lambda i: (l, 0, 0)),
        pl.BlockSpec((None, D_MODEL, D_IN), lambda i: (l, 0, 0)),
        pl.BlockSpec((None, 1, D_IN), lambda i: (l, 0, 0)),
        pl.BlockSpec((256, 256), lambda i: (0, 0)),
        pl.BlockSpec((256, 256), lambda i: (0, 0)),
    ]
    args = [x, mod4, g_attn.reshape(DEPTH, 1, D_MODEL), w_in_bf, gvec, bd64, bd32]
    if rope:
        nt = rope_tabs[0].shape[0] // tm
        in_specs += [pl.BlockSpec((tm, LANES), lambda i: (i % nt, 0))] * 4
        args += list(rope_tabs)
    out_shape = [jax.ShapeDtypeStruct((rows, D_IN), BF16)]
    out_specs = [pl.BlockSpec((tm, D_IN), lambda i: (i, 0))]
    aliases = {}
    n_alias = 0
    if caches is not None:
        per = tm // seq
        n_alias = len(caches)
        for n, w in enumerate((256, 256, 128, 128, 256, 256)):
            out_shape.append(jax.ShapeDtypeStruct((rows // seq, DEPTH, seq, w), F32))
            out_specs.append(pl.BlockSpec((per, None, seq, w), lambda i: (i, l, 0, 0)))
            if caches:
                aliases[len(args) + n] = 1 + n
        in_specs += [pl.BlockSpec(memory_space=pl.ANY)] * n_alias
        args += list(caches)
    return pl.pallas_call(
        functools.partial(_inproj_kernel, rope=rope, n_alias=n_alias),
        grid=grid,
        in_specs=in_specs,
        out_specs=out_specs,
        out_shape=out_shape,
        input_output_aliases=aliases,
        compiler_params=_cparams(("parallel",)),
        name="inproj_lat" if rope else "inproj_ctx",
    )(*args)


def _lane_masks():
    lane = lax.broadcasted_iota(jnp.int32, (1, LANES), 1)
    lo = lane < HEAD_DIM
    quarters = [(lane >= DA_DIM * j) & (lane < DA_DIM * (j + 1)) for j in range(4)]
    return lo, quarters


def _keep(q, mask):
    return jnp.where(mask, q, jnp.zeros_like(q))


def _attend(qm, ks, vs, biases=None, sink=None):
    if sink is not None:
        sink = sink * LOG2E
    ss = []
    for n, k in enumerate(ks):
        s = _dot_nt(qm, k)
        if biases is not None and biases[n] is not None:
            s = s + biases[n]
        ss.append(s)
    m = ss[0].max(axis=-1, keepdims=True)
    for s in ss[1:]:
        m = jnp.maximum(m, s.max(axis=-1, keepdims=True))
    if sink is not None:
        m = jnp.maximum(m, sink)
    l = None
    o = None
    for s, v in zip(ss, vs):
        p = jnp.exp2(s - m)
        ls = p.sum(axis=-1, keepdims=True)
        os = _dot(p.astype(BF16), v)
        l = ls if l is None else l + ls
        o = os if o is None else o + os
    if sink is not None:
        l = l + jnp.exp2(sink - m)
    return o * (1.0 / l)


def _diff_lambda(lam_ref, lam_init):
    lp = lam_ref[...]
    a = jnp.sum(lp[0:1] * lp[1:2], axis=-1, keepdims=True)
    b = jnp.sum(lp[2:3] * lp[3:4], axis=-1, keepdims=True)
    return jnp.exp(a) - jnp.exp(b) + lam_init


def _diff_pair(q, ks, vs, lam, subg, bd, lam_init, lo, quarters):
    o = [_attend(_keep(q, quarters[j]), ks, vs) for j in range(4)]
    d = jnp.where(lo, o[0] - lam * o[1], o[2] - lam * o[3])
    msq = _dot((d * d).astype(BF16), bd)
    return d * lax.rsqrt(msq + EPS) * subg * (1.0 - lam_init)


def _attn_ctx_kernel(qkv_ref, sink_ref, lam_ref, subg_ref, bd_ref, oa_ref, ob_ref, oc_ref, *, lam_init):
    lo, quarters = _lane_masks()
    hi = jnp.logical_not(lo)
    for pr in range(2):
        c = pr * LANES
        q = qkv_ref[:, QA + c:QA + c + LANES]
        k = [qkv_ref[:, KA + c:KA + c + LANES]]
        v = [qkv_ref[:, VA + c:VA + c + LANES]]
        o = jnp.where(lo, _attend(_keep(q, lo), k, v), _attend(_keep(q, hi), k, v))
        oa_ref[:, c:c + LANES] = o.astype(BF16)
    k = [qkv_ref[:, KB:KB + LANES]]
    v = [qkv_ref[:, VB:VB + LANES]]
    for pr in range(4):
        c = pr * LANES
        q = qkv_ref[:, QB + c:QB + c + LANES]
        o0 = _attend(_keep(q, lo), k, v, sink=sink_ref[_SW_ORDER[2 * pr]])
        o1 = _attend(_keep(q, hi), k, v, sink=sink_ref[_SW_ORDER[2 * pr + 1]])
        ob_ref[:, c:c + LANES] = jnp.where(lo, o0, o1).astype(BF16)
    lam = _diff_lambda(lam_ref, lam_init)
    for pr in range(2):
        c = pr * LANES
        q = qkv_ref[:, QC + c:QC + c + LANES]
        k = [qkv_ref[:, KC + c:KC + c + LANES]]
        v = [qkv_ref[:, VC + c:VC + c + LANES]]
        o = _diff_pair(q, k, v, lam, subg_ref[...], bd_ref[...], lam_init, lo, quarters)
        oc_ref[:, c:c + LANES] = o.astype(BF16)


def _attn_ctx(qkv, l, sw_sink, da_lambda, subg, bd128, lam_init):
    rows = qkv.shape[0]
    nb = rows // TQ
    return pl.pallas_call(
        functools.partial(_attn_ctx_kernel, lam_init=lam_init),
        grid=(nb,),
        in_specs=[
            pl.BlockSpec((TQ, D_IN), lambda b: (b, 0)),
            pl.BlockSpec(memory_space=pltpu.SMEM),
            pl.BlockSpec((None, 4, DA_DIM), lambda b: (l, 0, 0)),
            pl.BlockSpec((None, 1, LANES), lambda b: (l, 0, 0)),
            pl.BlockSpec((LANES, LANES), lambda b: (0, 0)),
        ],
        out_specs=[
            pl.BlockSpec((TQ, 256), lambda b: (b, 0)),
            pl.BlockSpec((TQ, 512), lambda b: (b, 0)),
            pl.BlockSpec((TQ, 256), lambda b: (b, 0)),
        ],
        out_shape=[
            jax.ShapeDtypeStruct((rows, 256), BF16),
            jax.ShapeDtypeStruct((rows, 512), BF16),
            jax.ShapeDtypeStruct((rows, 256), BF16),
        ],
        compiler_params=_cparams(("parallel",)),
        name="attn_ctx",
    )(qkv, sw_sink[l], da_lambda, subg, bd128)


NT_LAT = 2048 // TQ


def _attn_na_kernel(q_ref, kp_ref, kc_ref, kn_ref, vp_ref, vc_ref, vn_ref, ck_ref, cv_ref,
                    bp_ref, bc_ref, bn_ref, o_ref):
    lo, _ = _lane_masks()
    hi = jnp.logical_not(lo)
    ck = ck_ref[...].astype(BF16)
    cv = cv_ref[...].astype(BF16)
    for pr in range(2):
        c = pr * LANES
        sl = slice(c, c + LANES)
        q = q_ref[:, sl]
        ks = [kp_ref[:, sl], kc_ref[:, sl], kn_ref[:, sl], ck[:, sl]]
        vs = [vp_ref[:, sl], vc_ref[:, sl], vn_ref[:, sl], cv[:, sl]]
        outs = []
        for hl, mask in enumerate((lo, hi)):
            h = 2 * pr + hl
            outs.append(_attend(_keep(q, mask), ks, vs, biases=[bp_ref[h], bc_ref[h], bn_ref[h], None]))
        o_ref[:, sl] = jnp.where(lo, outs[0], outs[1]).astype(BF16)


def _attn_na(qkv, cache_k, cache_v, bias, l):
    rows = qkv.shape[0]
    nbatch = rows // 2048
    last = NT_LAT - 1

    def row(b, r):
        return b * NT_LAT + r

    def kv_specs(col):
        return [
            pl.BlockSpec((TQ, 256), lambda r, b: (row(b, jnp.maximum(r - 1, 0)), col)),
            pl.BlockSpec((TQ, 256), lambda r, b: (row(b, r), col)),
            pl.BlockSpec((TQ, 256), lambda r, b: (row(b, jnp.minimum(r + 1, last)), col)),
        ]

    def kind_prev(r):
        return jnp.where(r == 0, K_NONE, jnp.where(r == last, K_PREV_FULL, K_PREV))

    def kind_next(r):
        return jnp.where(r == 0, K_NEXT_FULL, jnp.where(r == last, K_NONE, K_NEXT))

    def bias_spec(kind_fn):
        return pl.BlockSpec((None, NA_HEADS, None, TQ, TQ), lambda r, b: (l, 0, kind_fn(r), 0, 0))

    cache_spec = pl.BlockSpec((None, None, 512, 256), lambda r, b: (b, l, 0, 0))
    return pl.pallas_call(
        _attn_na_kernel,
        grid=(NT_LAT, nbatch),
        in_specs=[pl.BlockSpec((TQ, 256), lambda r, b: (row(b, r), QA // 256))]
        + kv_specs(KA // 256) + kv_specs(VA // 256)
        + [cache_spec, cache_spec,
           bias_spec(kind_prev), bias_spec(lambda r: K_CUR), bias_spec(kind_next)],
        out_specs=pl.BlockSpec((TQ, 256), lambda r, b: (row(b, r), 0)),
        out_shape=jax.ShapeDtypeStruct((rows, 256), BF16),
        compiler_params=_cparams(("parallel", "parallel")),
        name="attn_lat_na",
    )(qkv, qkv, qkv, qkv, qkv, qkv, qkv, cache_k, cache_v, bias, bias, bias)


def _attn_sw_kernel(q_ref, k0_ref, k1_ref, k2_ref, k3_ref, v0_ref, v1_ref, v2_ref, v3_ref,
                    ck_ref, cv_ref, sink_ref, o_ref):
    r = pl.program_id(1)
    lo, _ = _lane_masks()
    hi = jnp.logical_not(lo)
    qi = lax.broadcasted_iota(jnp.int32, (TQ, TQ), 0)
    kj = lax.broadcasted_iota(jnp.int32, (TQ, TQ), 1)
    zero = jnp.zeros((TQ, TQ), F32)
    neg = jnp.full((TQ, TQ), NEG, F32)
    bias_a = jnp.where((kj >= qi) & ((r > 0) | (kj >= SW_WINDOW)), zero, neg)
    bias_b = jnp.where((kj <= qi) & ((r < NT_LAT - 1) | (kj < SW_WINDOW)), zero, neg)
    ks = [jnp.concatenate([k0_ref[...], k1_ref[...]], axis=0),
          jnp.concatenate([k2_ref[...], k3_ref[...]], axis=0), ck_ref[...].astype(BF16)]
    vs = [jnp.concatenate([v0_ref[...], v1_ref[...]], axis=0),
          jnp.concatenate([v2_ref[...], v3_ref[...]], axis=0), cv_ref[...].astype(BF16)]
    biases = [bias_a, bias_b, None]
    for pr in range(4):
        c = pr * LANES
        q = q_ref[:, c:c + LANES]
        o0 = _attend(_keep(q, lo), ks, vs, biases=biases, sink=sink_ref[_SW_ORDER[2 * pr]])
        o1 = _attend(_keep(q, hi), ks, vs, biases=biases, sink=sink_ref[_SW_ORDER[2 * pr + 1]])
        o_ref[:, c:c + LANES] = jnp.where(lo, o0, o1).astype(BF16)


def _attn_sw(qkv, cache_k, cache_v, sw_sink, l):
    rows = qkv.shape[0]
    nbatch = rows // 2048
    nblk = 2048 // SW_WINDOW

    def kv_specs(col):
        return [pl.BlockSpec((SW_WINDOW, LANES),
                             lambda b, r, t=t: (b * nblk + jnp.clip(2 * r - 1 + t, 0, nblk - 1), col))
                for t in range(4)]

    cache_spec = pl.BlockSpec((None, None, 512, LANES), lambda b, r: (b, l, 0, 0))
    return pl.pallas_call(
        _attn_sw_kernel,
        grid=(nbatch, NT_LAT),
        in_specs=[pl.BlockSpec((TQ, 512), lambda b, r: (b * NT_LAT + r, QB // 512))]
        + kv_specs(KB // LANES) + kv_specs(VB // LANES)
        + [cache_spec, cache_spec, pl.BlockSpec(memory_space=pltpu.SMEM)],
        out_specs=pl.BlockSpec((TQ, 512), lambda b, r: (b * NT_LAT + r, 0)),
        out_shape=jax.ShapeDtypeStruct((rows, 512), BF16),
        compiler_params=_cparams(("parallel", "parallel")),
        name="attn_lat_sw",
    )(*([qkv] * 9), cache_k, cache_v, sw_sink[l])


def _attn_da_kernel(q_ref, k_ref, v_ref, ck_ref, cv_ref, lam_ref, subg_ref, bd_ref, o_ref, *, lam_init):
    lo, quarters = _lane_masks()
    lam = _diff_lambda(lam_ref, lam_init)
    ks = [k_ref[...], ck_ref[...].astype(BF16)]
    vs = [v_ref[...], cv_ref[...].astype(BF16)]
    o = _diff_pair(q_ref[...], ks, vs, lam, subg_ref[...], bd_ref[...], lam_init, lo, quarters)
    o_ref[...] = o.astype(BF16)


def _attn_da(qkv, cache_k, cache_v, da_lambda, subg, bd128, l, lam_init):
    rows = qkv.shape[0]
    nbatch = rows // 2048
    cache_spec = pl.BlockSpec((None, None, 512, LANES), lambda b, p, r: (b, l, 0, p))
    nt = 2048 // TQ_DA
    return pl.pallas_call(
        functools.partial(_attn_da_kernel, lam_init=lam_init),
        grid=(nbatch, 2, nt),
        in_specs=[
            pl.BlockSpec((TQ_DA, LANES), lambda b, p, r: (b * nt + r, QC // LANES + p)),
            pl.BlockSpec((2048, LANES), lambda b, p, r: (b, KC // LANES + p)),
            pl.BlockSpec((2048, LANES), lambda b, p, r: (b, VC // LANES + p)),
            cache_spec, cache_spec,
            pl.BlockSpec((None, 4, DA_DIM), lambda b, p, r: (l, 0, 0)),
            pl.BlockSpec((None, 1, LANES), lambda b, p, r: (l, 0, 0)),
            pl.BlockSpec((LANES, LANES), lambda b, p, r: (0, 0)),
        ],
        out_specs=pl.BlockSpec((TQ_DA, LANES), lambda b, p, r: (b * nt + r, p)),
        out_shape=jax.ShapeDtypeStruct((rows, 256), BF16),
        compiler_params=_cparams(("parallel", "parallel", "parallel")),
        name="attn_lat_da",
    )(qkv, qkv, qkv, cache_k, cache_v, da_lambda, subg, bd128)


def _ffn_kernel(x_ref, xp_ref, xn_ref, oa_ref, oap_ref, oan_ref, ob_ref, obp_ref, obn_ref,
                oc_ref, ocp_ref, ocn_ref, mod_ref, g_ref, wo_ref, wup_ref, cw_ref, cb_ref, wd_ref,
                out_ref, x1_sc, h2_sc, gated_sc, ug0_sc, uv0_sc, ug1_sc, uv1_sc, *, seqlen):
    i = pl.program_id(0)
    tm = x_ref.shape[0]

    def prepare():
        gate1 = mod_ref[:, 2 * D_MODEL:3 * D_MODEL]
        shift2 = mod_ref[:, 3 * D_MODEL:4 * D_MODEL]
        scale2 = mod_ref[:, 4 * D_MODEL:5 * D_MODEL]

        def stage(xv, oa, ob, oc):
            mix = _dot(oa, wo_ref[0:256]) + _dot(ob, wo_ref[256:768]) + _dot(oc, wo_ref[768:1024])
            x1 = xv + gate1 * mix
            ms = jnp.mean(x1 * x1, axis=-1, keepdims=True)
            y = x1 * lax.rsqrt(ms + EPS) * g_ref[...]
            return x1, (y * (1.0 + scale2) + shift2).astype(BF16)

        for r0 in range(0, tm, FFN_STAGE_ROWS):
            r1 = r0 + FFN_STAGE_ROWS
            x1, h2 = stage(x_ref[r0:r1], oa_ref[r0:r1], ob_ref[r0:r1], oc_ref[r0:r1])
            x1_sc[r0:r1] = x1
            h2_sc[HALO + r0:HALO + r1] = h2
        h2_sc[0:HALO] = stage(xp_ref[...], oap_ref[...], obp_ref[...], ocp_ref[...])[1]
        h2_sc[HALO + tm:2 * HALO + tm] = stage(xn_ref[...], oan_ref[...], obn_ref[...], ocn_ref[...])[1]

    prepare()
    h2 = h2_sc[...]
    ck = CK_FFN
    mtot = tm + 2 * HALO
    sub = 8
    span = min(seqlen, tm)
    row8 = lax.broadcasted_iota(jnp.int32, (sub, ck), 0)

    def zero_rows(v, group_starts, edge):
        pieces = []
        cur = 0
        for g0 in group_starts:
            if g0 > cur:
                pieces.append(v[cur:g0])
            pos = (i * tm + g0 + row8) & (seqlen - 1)
            pieces.append(jnp.where(pos != edge, v[g0:g0 + sub], 0.0))
            cur = g0 + sub
        if cur < tm:
            pieces.append(v[cur:])
        return jnp.concatenate(pieces, axis=0)

    first_groups = list(range(0, tm, span))
    last_groups = [g + span - sub for g in first_groups]

    def conv(u, col):
        cw = cw_ref[:, col:col + ck]
        up = zero_rows(pltpu.roll(u, 1, 0)[HALO:HALO + tm], first_groups, 0)
        un = zero_rows(pltpu.roll(u, mtot - 1, 0)[HALO:HALO + tm], last_groups, seqlen - 1)
        y = cb_ref[:, col:col + ck] + up * cw[0:1]
        y = y + u[HALO:HALO + tm] * cw[1:2]
        return y + un * cw[2:3]

    def silu_mul(g, v):
        h = 0.5 * g
        return (h * jnp.tanh(h) + h) * v

    nj = D_FF // ck
    bufs = ((ug0_sc, uv0_sc), (ug1_sc, uv1_sc))

    def up_project(c):
        ug_sc, uv_sc = bufs[c % 2]
        ug_sc[...] = _dot(h2, wup_ref[:, c * ck:(c + 1) * ck])
        uv_sc[...] = _dot(h2, wup_ref[:, D_FF + c * ck:D_FF + (c + 1) * ck])

    def gate(c):
        ug_sc, uv_sc = bufs[c % 2]
        yg = conv(ug_sc[...], c * ck)
        yv = conv(uv_sc[...], D_FF + c * ck)
        gated_sc[:, c * ck:(c + 1) * ck] = silu_mul(yg, yv).astype(BF16)

    up_project(0)
    for c in range(nj):
        if c + 1 < nj:
            up_project(c + 1)
        gate(c)
    ffn = _dot(gated_sc[...], wd_ref[...])
    out_ref[...] = x1_sc[...] + mod_ref[:, 5 * D_MODEL:6 * D_MODEL] * ffn


def _ffn(x, oa, ob, oc, mod4, mod_row_fn, l, g_ffn, w_out_bf, w_up_bf, conv_w, conv_b, w_down_bf, seqlen):
    rows = x.shape[0]
    tm = TM_FFN
    nhb = rows // HALO
    per = tm // HALO

    def tile(w):
        return pl.BlockSpec((tm, w), lambda i: (i, 0))

    def prev(w):
        return pl.BlockSpec((HALO, w), lambda i: (jnp.maximum(i * per - 1, 0), 0))

    def nxt(w):
        return pl.BlockSpec((HALO, w), lambda i: (jnp.minimum((i + 1) * per, nhb - 1), 0))

    def trio(w):
        return [tile(w), prev(w), nxt(w)]

    def resident(shape):
        return pl.BlockSpec((None,) + shape, lambda i: (l, 0, 0), pipeline_mode=pl.Buffered(1))

    in_specs = trio(D_MODEL) + trio(256) + trio(512) + trio(256) + [
        pl.BlockSpec((None, None, 1, 6 * D_MODEL), lambda i: (l, mod_row_fn(i, tm), 0, 0)),
        resident((1, D_MODEL)),
        resident((D_MIX, D_MODEL)),
        resident((D_MODEL, 2 * D_FF)),
        resident((3, 2 * D_FF)),
        resident((1, 2 * D_FF)),
        resident((D_FF, D_MODEL)),
    ]
    return pl.pallas_call(
        functools.partial(_ffn_kernel, seqlen=seqlen),
        grid=(rows // tm,),
        in_specs=in_specs,
        out_specs=pl.BlockSpec((tm, D_MODEL), lambda i: (i, 0)),
        out_shape=jax.ShapeDtypeStruct((rows, D_MODEL), F32),
        scratch_shapes=[
            pltpu.VMEM((tm, D_MODEL), F32),
            pltpu.VMEM((tm + 2 * HALO, D_MODEL), BF16),
            pltpu.VMEM((tm, D_FF), BF16),
        ] + [pltpu.VMEM((tm + 2 * HALO, CK_FFN), F32)] * 4,
        compiler_params=_cparams(("parallel",)),
        name="outproj_ffn",
    )(x, x, x, oa, oa, oa, ob, ob, ob, oc, oc, oc, mod4, g_ffn.reshape(DEPTH, 1, D_MODEL),
      w_out_bf, w_up_bf, conv_w, conv_b.reshape(DEPTH, 1, 2 * D_FF), w_down_bf)


def _block_diag_mean(n, group):
    idx = np.arange(n) // group
    return jnp.asarray((idx[:, None] == idx[None, :]).astype(np.float32) / group, dtype=BF16)


def _rope_tables(T, dim):
    n = dim // 4
    inv = 1.0 / (ROPE_BASE ** (jnp.arange(n, dtype=F32) / n))
    t = jnp.arange(T)
    rowp = (t // GRID_W).astype(F32)
    colp = (t % GRID_W).astype(F32)
    ang = jnp.concatenate([rowp[:, None] * inv, colp[:, None] * inv], axis=-1)
    cos = jnp.repeat(jnp.cos(ang), 2, axis=-1)
    sin = jnp.repeat(jnp.sin(ang), 2, axis=-1) * jnp.tile(jnp.asarray([-1.0, 1.0], F32), dim // 2)
    reps = LANES // dim
    return jnp.tile(cos, (1, reps)), jnp.tile(sin, (1, reps))


def _pair_sw_heads(w, axis):
    assert _SW_ORDER == tuple(g * 4 + i for i in range(4) for g in range(2))
    shp = w.shape
    w = w.reshape(shp[:axis] + (2, 4, HEAD_DIM) + shp[axis + 1:])
    return jnp.swapaxes(w, axis, axis + 1).reshape(shp)


def kernel(x_prompt, x_sample, cache_na_k, cache_na_v, cache_sw_k, cache_sw_v, cache_da_k, cache_da_v, c, c_ctx, g_attn, g_ffn, w_mod, b_mod, w_in, na_qk_g, na_rpb, sw_qk_g, sw_sink, da_qk_g, da_lambda, da_subln_g, w_out, w_up, conv_w, conv_b, w_down):
    nb_ctx, seq, _ = x_prompt.shape
    nb_lat, T, _ = x_sample.shape
    npast = cache_na_k.shape[2]

    w_in_bf = jnp.concatenate([
        _pair_sw_heads(w_in[:, :, 768:1280], 2), w_in[:, :, 0:768], w_in[:, :, 1280:]], axis=2).astype(BF16)
    w_out_bf = jnp.concatenate([
        w_out[:, 0:256], _pair_sw_heads(w_out[:, 256:768], 1), w_out[:, 768:]], axis=1).astype(BF16)
    w_up_bf = w_up.astype(BF16)
    w_down_bf = w_down.astype(BF16)

    ones = lambda n: jnp.ones((DEPTH, n), F32)
    sc64 = HEAD_DIM ** -0.5 * LOG2E
    sc32 = DA_DIM ** -0.5 * LOG2E
    gvec = jnp.concatenate([
        jnp.tile(sw_qk_g[:, 0], (1, 8)) * sc64,
        jnp.tile(na_qk_g[:, 0], (1, 4)) * sc64, jnp.tile(na_qk_g[:, 1], (1, 4)), ones(256),
        jnp.tile(sw_qk_g[:, 1], (1, 2)), ones(128),
        jnp.tile(da_qk_g[:, 0], (1, 8)) * sc32, jnp.tile(da_qk_g[:, 1], (1, 8)), ones(256),
    ], axis=-1).reshape(DEPTH, 1, D_IN)
    subg = jnp.tile(da_subln_g, (1, 2)).reshape(DEPTH, 1, LANES)
    bd64 = _block_diag_mean(256, HEAD_DIM)
    bd32 = _block_diag_mean(256, DA_DIM)
    bd128 = _block_diag_mean(LANES, HEAD_DIM)
    rope_tabs = _rope_tables(T, HEAD_DIM) + _rope_tables(T, DA_DIM)

    cna_k = cache_na_k.reshape(nb_lat, DEPTH, npast, 256)
    cna_v = cache_na_v.reshape(nb_lat, DEPTH, npast, 256)
    csw_k = cache_sw_k.reshape(nb_lat, DEPTH, npast, 128)
    csw_v = cache_sw_v.reshape(nb_lat, DEPTH, npast, 128)
    cda_k = cache_da_k.reshape(nb_lat, DEPTH, npast, 256)
    cda_v = cache_da_v.reshape(nb_lat, DEPTH, npast, 256)

    cvecs = jnp.concatenate([c_ctx[None, :], c, jnp.zeros((8 - 1 - nb_lat, D_MODEL), F32)], axis=0)
    mod4 = _modulation(cvecs, w_mod, b_mod).reshape(DEPTH, 8, 1, 6 * D_MODEL)
    bias = _na_bias_tiles(na_rpb)

    ctx_row = lambda i, tm: 0
    lat_row = lambda i, tm: 1 + (i * tm) // T

    xp = x_prompt.reshape(nb_ctx * seq, D_MODEL)
    xs = x_sample.reshape(nb_lat * T, D_MODEL)
    caches = ()
    for l in range(DEPTH):
        lam_init = 0.8 - 0.6 * math.exp(-0.3 * l)
        qkv_p, *caches = _inproj(xp, mod4, ctx_row, l, g_attn, w_in_bf, gvec, bd64, bd32, None, caches, seq)
        oa, ob, oc = _attn_ctx(qkv_p, l, sw_sink, da_lambda, subg, bd128, lam_init)
        xp = _ffn(xp, oa, ob, oc, mod4, ctx_row, l, g_ffn, w_out_bf, w_up_bf, conv_w, conv_b, w_down_bf, seq)
        qkv_s = _inproj(xs, mod4, lat_row, l, g_attn, w_in_bf, gvec, bd64, bd32, rope_tabs, None, T)[0]
        oa = _attn_na(qkv_s, cna_k, cna_v, bias, l)
        ob = _attn_sw(qkv_s, csw_k, csw_v, sw_sink, l)
        oc = _attn_da(qkv_s, cda_k, cda_v, da_lambda, subg, bd128, l, lam_init)
        xs = _ffn(xs, oa, ob, oc, mod4, lat_row, l, g_ffn, w_out_bf, w_up_bf, conv_w, conv_b, w_down_bf, T)

    def heads(a, tail):
        return a.reshape((nb_ctx, DEPTH, seq) + tail)

    return (xp.reshape(nb_ctx, seq, D_MODEL), xs.reshape(nb_lat, T, D_MODEL),
            heads(caches[0], (NA_HEADS, HEAD_DIM)), heads(caches[1], (NA_HEADS, HEAD_DIM)),
            heads(caches[2], (SW_KV_HEADS, HEAD_DIM)), heads(caches[3], (SW_KV_HEADS, HEAD_DIM)),
            heads(caches[4], (DA_HEADS, 2, DA_DIM)), heads(caches[5], (DA_HEADS, DA_DIM * 2)))
```

```python
import functools
import math

import numpy as np
import jax
import jax.numpy as jnp
from jax import lax
from jax.experimental import pallas as pl
from jax.experimental.pallas import tpu as pltpu

F32 = jnp.float32
BF16 = jnp.bfloat16

D_MODEL = 1024
DEPTH = 4
GRID_W = 64
HEAD_DIM = 64
NA_HEADS = 4
NA_WIN_ROWS = 8
NA_WIN_COLS = 16
SW_HEADS = 8
SW_KV_HEADS = 2
SW_WINDOW = 128
DA_HEADS = 4
DA_DIM = 32
D_FF = 2816
ROPE_BASE = 10000.0
EPS = 1e-6

QB = 0
QA, KA, VA = 512, 768, 1024
KB, VB = 1280, 1408
QC, KC, VC = 1536, 1792, 2048
D_IN = 2304
D_MIX = 1024
_SW_ORDER = (0, 4, 1, 5, 2, 6, 3, 7)

LANES = 128
NEG = -1e30
LOG2E = math.log2(math.e)
TQ = 256
TQ_DA = 1024
CTX_SEQS = 2
LAT_BATCHES = 2
TM_IN = 1024
TM_FFN = 512
HALO = 16
CK_FFN = 256
FFN_STAGE_ROWS = 256
VMEM_LIMIT = 56 * 1024 * 1024


def _cparams(sem):
    return pltpu.CompilerParams(dimension_semantics=sem, vmem_limit_bytes=VMEM_LIMIT)


def _dot(a, b):
    return jnp.dot(a, b, preferred_element_type=F32)


def _dot_nt(a, b):
    return lax.dot_general(a, b, (((1,), (1,)), ((), ())), preferred_element_type=F32)


def _sigmoid(x):
    return 1.0 / (1.0 + jnp.exp(-x))


def _mod_kernel(c_ref, w_ref, b_ref, o_ref):
    c = c_ref[...]
    s = (c * _sigmoid(c)).astype(BF16)
    o_ref[...] = _dot(s, w_ref[...].astype(BF16)) + b_ref[...]


def _modulation(cvecs, w_mod, b_mod):
    nchunk = 4
    cn = 6 * D_MODEL // nchunk
    return pl.pallas_call(
        _mod_kernel,
        grid=(DEPTH, nchunk),
        in_specs=[
            pl.BlockSpec((8, D_MODEL), lambda l, n: (0, 0)),
            pl.BlockSpec((None, D_MODEL, cn), lambda l, n: (l, 0, n)),
            pl.BlockSpec((None, 1, cn), lambda l, n: (l, 0, n)),
        ],
        out_specs=pl.BlockSpec((None, 8, cn), lambda l, n: (l, 0, n)),
        out_shape=jax.ShapeDtypeStruct((DEPTH, 8, 6 * D_MODEL), F32),
        compiler_params=_cparams(("arbitrary", "arbitrary")),
        name="modulation",
    )(cvecs, w_mod, b_mod.reshape(DEPTH, 1, 6 * D_MODEL))


N_RPB_COLS = 2 * NA_WIN_COLS - 1
_BIAS_KINDS = (
    (3, lambda i, j: j >= i),
    (7, lambda i, j: True),
    (11, lambda i, j: j < i),
    (3, lambda i, j: True),
    (11, lambda i, j: True),
    (0, lambda i, j: False),
)
K_PREV, K_CUR, K_NEXT, K_PREV_FULL, K_NEXT_FULL, K_NONE = range(6)


def _bias_kernel(rpb_ref, out_ref):
    lh = pl.program_id(0)
    qcol = lax.broadcasted_iota(jnp.int32, (GRID_W, LANES), 0)
    lane = lax.broadcasted_iota(jnp.int32, (GRID_W, LANES), 1)
    kcol = lane & (GRID_W - 1)
    hi = lane >= GRID_W
    f = jnp.clip(kcol - qcol, -(NA_WIN_COLS - 1), NA_WIN_COLS - 1) + (NA_WIN_COLS - 1)
    qs = jnp.clip(qcol - NA_WIN_COLS // 2, 0, GRID_W - NA_WIN_COLS)
    cvalid = (kcol >= qs) & (kcol < qs + NA_WIN_COLS)
    masked = jnp.full((GRID_W, LANES), NEG, F32)

    @functools.lru_cache(maxsize=None)
    def block_pair(a0, a1):
        if a0 is None and a1 is None:
            return masked
        acc = jnp.zeros((GRID_W, LANES), F32)
        for b in range(N_RPB_COLS):
            s0 = rpb_ref[lh, a0 * N_RPB_COLS + b] if a0 is not None else 0.0
            s1 = rpb_ref[lh, a1 * N_RPB_COLS + b] if a1 is not None else 0.0
            acc = jnp.where(f == b, jnp.where(hi, s1, s0), acc)
        valid = cvalid
        if a0 is None:
            valid = valid & hi
        if a1 is None:
            valid = valid & jnp.logical_not(hi)
        return jnp.where(valid, acc * LOG2E, masked)

    for kind, (off, inside) in enumerate(_BIAS_KINDS):
        for i in range(4):
            for jt in range(2):
                a = [(j - i + off) if inside(i, j) else None for j in (2 * jt, 2 * jt + 1)]
                out_ref[kind, i * GRID_W:(i + 1) * GRID_W, jt * LANES:(jt + 1) * LANES] = block_pair(a[0], a[1])


def _na_bias_tiles(na_rpb):
    n = DEPTH * NA_HEADS
    rpb2 = na_rpb.reshape(n, (2 * NA_WIN_ROWS - 1) * N_RPB_COLS)
    out = pl.pallas_call(
        _bias_kernel,
        grid=(n,),
        in_specs=[pl.BlockSpec(memory_space=pltpu.SMEM)],
        out_specs=pl.BlockSpec((None, len(_BIAS_KINDS), TQ, TQ), lambda i: (i, 0, 0, 0)),
        out_shape=jax.ShapeDtypeStruct((n, len(_BIAS_KINDS), TQ, TQ), F32),
        compiler_params=_cparams(("arbitrary",)),
        name="na_bias_tiles",
    )(rpb2)
    return out.reshape(DEPTH, NA_HEADS, len(_BIAS_KINDS), TQ, TQ)


_NORM64 = ((QB, 256), (QB + 256, 256), (QA, 256), (KA, 256), (KB, 128))
_NORM32 = ((QC, 256), (KC, 256))
_PLAIN = ((VA, 256), (VB, 128), (VC, 256))
_ROPE_B = (QB, QB + 256, KB)
_ROPE_C = (QC, KC)
_CACHE_COLS = (KA, VA, KB, VB, KC, VC)


def _inproj_kernel(*refs, rope, n_alias):
    x_ref, mod_ref, g_ref, w_ref, gvec_ref, bd64_ref, bd32_ref = refs[:7]
    rest = refs[7:]
    if rope:
        cosb_ref, sinb_ref, cosc_ref, sinc_ref = rest[:4]
        rest = rest[4:]
    rest = rest[n_alias:]
    qkv_ref = rest[0]
    cache_refs = rest[1:]

    x = x_ref[...]
    ms = jnp.mean(x * x, axis=-1, keepdims=True)
    y = x * lax.rsqrt(ms + EPS) * g_ref[...]
    h = (y * (1.0 + mod_ref[:, D_MODEL:2 * D_MODEL]) + mod_ref[:, 0:D_MODEL]).astype(BF16)
    tm = x.shape[0]
    even = (lax.broadcasted_iota(jnp.int32, (tm, LANES), 1) & 1) == 0

    def rotary(t, cos_ref, sin_ref):
        cos = cos_ref[...]
        sin = sin_ref[...]
        parts = []
        for c in range(t.shape[1] // LANES):
            tc = t[:, c * LANES:(c + 1) * LANES]
            partner = jnp.where(even, pltpu.roll(tc, LANES - 1, 1), pltpu.roll(tc, 1, 1))
            parts.append(tc * cos + partner * sin)
        return parts[0] if len(parts) == 1 else jnp.concatenate(parts, axis=1)

    cache_of = dict(zip(_CACHE_COLS, cache_refs))

    def emit(a, w, t):
        qkv_ref[:, a:a + w] = t.astype(BF16)
        if a in cache_of:
            ref = cache_of[a]
            seq = ref.shape[1]
            for n in range(ref.shape[0]):
                ref[n] = t[n * seq:(n + 1) * seq]

    p = _dot(h, w_ref[...])
    for segs, bd_ref in ((_NORM64, bd64_ref), (_NORM32, bd32_ref)):
        for a, w in segs:
            t = p[:, a:a + w]
            msq = _dot((t * t).astype(BF16), bd_ref[0:w, 0:w])
            t = t * lax.rsqrt(msq + EPS) * gvec_ref[:, a:a + w]
            if rope and a in _ROPE_B:
                t = rotary(t, cosb_ref, sinb_ref)
            if rope and a in _ROPE_C:
                t = rotary(t, cosc_ref, sinc_ref)
            emit(a, w, t)
    for a, w in _PLAIN:
        emit(a, w, p[:, a:a + w])


def _inproj(x, mod4, mod_row_fn, l, g_attn, w_in_bf, gvec, bd64, bd32, rope_tabs, caches, seq):
    rows = x.shape[0]
    tm = TM_IN
    grid = (rows // tm,)
    rope = rope_tabs is not None
    in_specs = [
        pl.BlockSpec((tm, D_MODEL), lambda i: (i, 0)),
        pl.BlockSpec((None, None, 1, 6 * D_MODEL), lambda i: (l, mod_row_fn(i, tm), 0, 0)),
        pl.BlockSpec((None, 1, D_MODEL), lambda i: (l, 0, 0)),
        pl.BlockSpec((None, D_MODEL, D_IN), lambda i: (l, 0, 0), pipeline_mode=pl.Buffered(1)),
        pl.BlockSpec((None, 1, D_IN), lambda i: (l, 0, 0)),
        pl.BlockSpec((256, 256), lambda i: (0, 0)),
        pl.BlockSpec((256, 256), lambda i: (0, 0)),
    ]
    args = [x, mod4, g_attn.reshape(DEPTH, 1, D_MODEL), w_in_bf, gvec, bd64, bd32]
    if rope:
        nt = rope_tabs[0].shape[0] // tm
        in_specs += [pl.BlockSpec((tm, LANES), lambda i: (i % nt, 0))] * 4
        args += list(rope_tabs)
    out_shape = [jax.ShapeDtypeStruct((rows, D_IN), BF16)]
    out_specs = [pl.BlockSpec((tm, D_IN), lambda i: (i, 0))]
    aliases = {}
    n_alias = 0
    if caches is not None:
        per = tm // seq
        n_alias = len(caches)
        for n, w in enumerate((256, 256, 128, 128, 256, 256)):
            out_shape.append(jax.ShapeDtypeStruct((rows // seq, DEPTH, seq, w), F32))
            out_specs.append(pl.BlockSpec((per, None, seq, w), lambda i: (i, l, 0, 0)))
            if caches:
                aliases[len(args) + n] = 1 + n
        in_specs += [pl.BlockSpec(memory_space=pl.ANY)] * n_alias
        args += list(caches)
    return pl.pallas_call(
        functools.partial(_inproj_kernel, rope=rope, n_alias=n_alias),
        grid=grid,
        in_specs=in_specs,
        out_specs=out_specs,
        out_shape=out_shape,
        input_output_aliases=aliases,
        compiler_params=_cparams(("parallel",)),
        name="inproj_lat" if rope else "inproj_ctx",
    )(*args)


def _lane_masks():
    lane = lax.broadcasted_iota(jnp.int32, (1, LANES), 1)
    lo = lane < HEAD_DIM
    quarters = [(lane >= DA_DIM * j) & (lane < DA_DIM * (j + 1)) for j in range(4)]
    return lo, quarters


def _keep(q, mask):
    return jnp.where(mask, q, jnp.zeros_like(q))


def _attend(qm, ks, vs, biases=None, sink=None):
    if sink is not None:
        sink = sink * LOG2E
    ss = []
    for n, k in enumerate(ks):
        s = _dot_nt(qm, k)
        if biases is not None and biases[n] is not None:
            s = s + biases[n]
        ss.append(s)
    m = ss[0].max(axis=-1, keepdims=True)
    for s in ss[1:]:
        m = jnp.maximum(m, s.max(axis=-1, keepdims=True))
    if sink is not None:
        m = jnp.maximum(m, sink)
    l = None
    o = None
    for s, v in zip(ss, vs):
        p = jnp.exp2(s - m)
        ls = p.sum(axis=-1, keepdims=True)
        os = _dot(p.astype(BF16), v)
        l = ls if l is None else l + ls
        o = os if o is None else o + os
    if sink is not None:
        l = l + jnp.exp2(sink - m)
    return o * (1.0 / l)


def _diff_lambda(lam_ref, lam_init):
    lp = lam_ref[...]
    a = jnp.sum(lp[0:1] * lp[1:2], axis=-1, keepdims=True)
    b = jnp.sum(lp[2:3] * lp[3:4], axis=-1, keepdims=True)
    return jnp.exp(a) - jnp.exp(b) + lam_init


def _diff_pair(q, ks, vs, lam, subg, bd, lam_init, lo, quarters):
    o = [_attend(_keep(q, quarters[j]), ks, vs) for j in range(4)]
    d = jnp.where(lo, o[0] - lam * o[1], o[2] - lam * o[3])
    msq = _dot((d * d).astype(BF16), bd)
    return d * lax.rsqrt(msq + EPS) * subg * (1.0 - lam_init)


def _attn_ctx_kernel(qkv_ref, sink_ref, lam_ref, subg_ref, bd_ref, oa_ref, ob_ref, oc_ref, *, lam_init):
    lo, quarters = _lane_masks()
    hi = jnp.logical_not(lo)
    lam = _diff_lambda(lam_ref, lam_init)
    for n in range(qkv_ref.shape[0] // TQ):
        rows = slice(n * TQ, (n + 1) * TQ)
        for pr in range(2):
            c = pr * LANES
            q = qkv_ref[rows, QA + c:QA + c + LANES]
            k = [qkv_ref[rows, KA + c:KA + c + LANES]]
            v = [qkv_ref[rows, VA + c:VA + c + LANES]]
            o = jnp.where(lo, _attend(_keep(q, lo), k, v), _attend(_keep(q, hi), k, v))
            oa_ref[rows, c:c + LANES] = o.astype(BF16)
        k = [qkv_ref[rows, KB:KB + LANES]]
        v = [qkv_ref[rows, VB:VB + LANES]]
        for pr in range(4):
            c = pr * LANES
            q = qkv_ref[rows, QB + c:QB + c + LANES]
            o0 = _attend(_keep(q, lo), k, v, sink=sink_ref[_SW_ORDER[2 * pr]])
            o1 = _attend(_keep(q, hi), k, v, sink=sink_ref[_SW_ORDER[2 * pr + 1]])
            ob_ref[rows, c:c + LANES] = jnp.where(lo, o0, o1).astype(BF16)
        for pr in range(2):
            c = pr * LANES
            q = qkv_ref[rows, QC + c:QC + c + LANES]
            k = [qkv_ref[rows, KC + c:KC + c + LANES]]
            v = [qkv_ref[rows, VC + c:VC + c + LANES]]
            o = _diff_pair(q, k, v, lam, subg_ref[...], bd_ref[...], lam_init, lo, quarters)
            oc_ref[rows, c:c + LANES] = o.astype(BF16)


def _attn_ctx(qkv, l, sw_sink, da_lambda, subg, bd128, lam_init):
    rows = qkv.shape[0]
    tr = CTX_SEQS * TQ
    return pl.pallas_call(
        functools.partial(_attn_ctx_kernel, lam_init=lam_init),
        grid=(rows // tr,),
        in_specs=[
            pl.BlockSpec((tr, D_IN), lambda b: (b, 0)),
            pl.BlockSpec(memory_space=pltpu.SMEM),
            pl.BlockSpec((None, 4, DA_DIM), lambda b: (l, 0, 0)),
            pl.BlockSpec((None, 1, LANES), lambda b: (l, 0, 0)),
            pl.BlockSpec((LANES, LANES), lambda b: (0, 0)),
        ],
        out_specs=[
            pl.BlockSpec((tr, 256), lambda b: (b, 0)),
            pl.BlockSpec((tr, 512), lambda b: (b, 0)),
            pl.BlockSpec((tr, 256), lambda b: (b, 0)),
        ],
        out_shape=[
            jax.ShapeDtypeStruct((rows, 256), BF16),
            jax.ShapeDtypeStruct((rows, 512), BF16),
            jax.ShapeDtypeStruct((rows, 256), BF16),
        ],
        compiler_params=_cparams(("parallel",)),
        name="attn_ctx",
    )(qkv, sw_sink[l], da_lambda, subg, bd128)


NT_LAT = 2048 // TQ


def _attn_na_kernel(q_ref, kp_ref, kc_ref, kn_ref, vp_ref, vc_ref, vn_ref, ck_ref, cv_ref,
                    bp_ref, bc_ref, bn_ref, o_ref):
    lo, _ = _lane_masks()
    hi = jnp.logical_not(lo)
    for n in range(q_ref.shape[0]):
        ck = ck_ref[n].astype(BF16)
        cv = cv_ref[n].astype(BF16)
        for pr in range(2):
            c = pr * LANES
            sl = slice(c, c + LANES)
            q = q_ref[n, :, sl]
            ks = [kp_ref[n, :, sl], kc_ref[n, :, sl], kn_ref[n, :, sl], ck[:, sl]]
            vs = [vp_ref[n, :, sl], vc_ref[n, :, sl], vn_ref[n, :, sl], cv[:, sl]]
            outs = []
            for hl, mask in enumerate((lo, hi)):
                h = 2 * pr + hl
                outs.append(_attend(_keep(q, mask), ks, vs, biases=[bp_ref[h], bc_ref[h], bn_ref[h], None]))
            o_ref[n, :, sl] = jnp.where(lo, outs[0], outs[1]).astype(BF16)


def _attn_na(qkv, cache_k, cache_v, bias, l):
    nbatch, _, _ = qkv.shape
    nb = LAT_BATCHES
    last = NT_LAT - 1

    def kv_specs(col):
        return [
            pl.BlockSpec((nb, TQ, 256), lambda r, b: (b, jnp.maximum(r - 1, 0), col)),
            pl.BlockSpec((nb, TQ, 256), lambda r, b: (b, r, col)),
            pl.BlockSpec((nb, TQ, 256), lambda r, b: (b, jnp.minimum(r + 1, last), col)),
        ]

    def kind_prev(r):
        return jnp.where(r == 0, K_NONE, jnp.where(r == last, K_PREV_FULL, K_PREV))

    def kind_next(r):
        return jnp.where(r == 0, K_NEXT_FULL, jnp.where(r == last, K_NONE, K_NEXT))

    def bias_spec(kind_fn):
        return pl.BlockSpec((None, NA_HEADS, None, TQ, TQ), lambda r, b: (l, 0, kind_fn(r), 0, 0))

    cache_spec = pl.BlockSpec((nb, None, 512, 256), lambda r, b: (b, l, 0, 0))
    return pl.pallas_call(
        _attn_na_kernel,
        grid=(NT_LAT, nbatch // nb),
        in_specs=[pl.BlockSpec((nb, TQ, 256), lambda r, b: (b, r, QA // 256))]
        + kv_specs(KA // 256) + kv_specs(VA // 256)
        + [cache_spec, cache_spec,
           bias_spec(kind_prev), bias_spec(lambda r: K_CUR), bias_spec(kind_next)],
        out_specs=pl.BlockSpec((nb, TQ, 256), lambda r, b: (b, r, 0)),
        out_shape=jax.ShapeDtypeStruct((nbatch, NT_LAT * TQ, 256), BF16),
        compiler_params=_cparams(("parallel", "parallel")),
        name="attn_lat_na",
    )(qkv, qkv, qkv, qkv, qkv, qkv, qkv, cache_k, cache_v, bias, bias, bias)


def _attn_sw_kernel(q_ref, k0_ref, k1_ref, k2_ref, k3_ref, v0_ref, v1_ref, v2_ref, v3_ref,
                    ck_ref, cv_ref, sink_ref, o_ref):
    r = pl.program_id(1)
    lo, _ = _lane_masks()
    hi = jnp.logical_not(lo)
    qi = lax.broadcasted_iota(jnp.int32, (TQ, TQ), 0)
    kj = lax.broadcasted_iota(jnp.int32, (TQ, TQ), 1)
    zero = jnp.zeros((TQ, TQ), F32)
    neg = jnp.full((TQ, TQ), NEG, F32)
    bias_a = jnp.where((kj >= qi) & ((r > 0) | (kj >= SW_WINDOW)), zero, neg)
    bias_b = jnp.where((kj <= qi) & ((r < NT_LAT - 1) | (kj < SW_WINDOW)), zero, neg)
    biases = [bias_a, bias_b, None]
    for n in range(q_ref.shape[0]):
        ks = [jnp.concatenate([k0_ref[n], k1_ref[n]], axis=0),
              jnp.concatenate([k2_ref[n], k3_ref[n]], axis=0), ck_ref[n].astype(BF16)]
        vs = [jnp.concatenate([v0_ref[n], v1_ref[n]], axis=0),
              jnp.concatenate([v2_ref[n], v3_ref[n]], axis=0), cv_ref[n].astype(BF16)]
        for pr in range(4):
            c = pr * LANES
            q = q_ref[n, :, c:c + LANES]
            o0 = _attend(_keep(q, lo), ks, vs, biases=biases, sink=sink_ref[_SW_ORDER[2 * pr]])
            o1 = _attend(_keep(q, hi), ks, vs, biases=biases, sink=sink_ref[_SW_ORDER[2 * pr + 1]])
            o_ref[n, :, c:c + LANES] = jnp.where(lo, o0, o1).astype(BF16)


def _attn_sw(qkv, cache_k, cache_v, sw_sink, l):
    nbatch, T, _ = qkv.shape
    nb = LAT_BATCHES
    nblk = T // SW_WINDOW

    def kv_specs(col):
        return [pl.BlockSpec((nb, SW_WINDOW, LANES),
                             lambda b, r, t=t: (b, jnp.clip(2 * r - 1 + t, 0, nblk - 1), col))
                for t in range(4)]

    cache_spec = pl.BlockSpec((nb, None, 512, LANES), lambda b, r: (b, l, 0, 0))
    return pl.pallas_call(
        _attn_sw_kernel,
        grid=(nbatch // nb, NT_LAT),
        in_specs=[pl.BlockSpec((nb, TQ, 512), lambda b, r: (b, r, QB // 512))]
        + kv_specs(KB // LANES) + kv_specs(VB // LANES)
        + [cache_spec, cache_spec, pl.BlockSpec(memory_space=pltpu.SMEM)],
        out_specs=pl.BlockSpec((nb, TQ, 512), lambda b, r: (b, r, 0)),
        out_shape=jax.ShapeDtypeStruct((nbatch, T, 512), BF16),
        compiler_params=_cparams(("parallel", "parallel")),
        name="attn_lat_sw",
    )(*([qkv] * 9), cache_k, cache_v, sw_sink[l])


def _attn_da_kernel(q_ref, k_ref, v_ref, ck_ref, cv_ref, lam_ref, subg_ref, bd_ref, o_ref, *, lam_init):
    lo, quarters = _lane_masks()
    lam = _diff_lambda(lam_ref, lam_init)
    ks = [k_ref[...], ck_ref[...].astype(BF16)]
    vs = [v_ref[...], cv_ref[...].astype(BF16)]
    o = _diff_pair(q_ref[...], ks, vs, lam, subg_ref[...], bd_ref[...], lam_init, lo, quarters)
    o_ref[...] = o.astype(BF16)


def _attn_da(qkv, cache_k, cache_v, da_lambda, subg, bd128, l, lam_init):
    rows = qkv.shape[0]
    nbatch = rows // 2048
    cache_spec = pl.BlockSpec((None, None, 512, LANES), lambda b, p, r: (b, l, 0, p))
    nt = 2048 // TQ_DA
    return pl.pallas_call(
        functools.partial(_attn_da_kernel, lam_init=lam_init),
        grid=(nbatch, 2, nt),
        in_specs=[
            pl.BlockSpec((TQ_DA, LANES), lambda b, p, r: (b * nt + r, QC // LANES + p)),
            pl.BlockSpec((2048, LANES), lambda b, p, r: (b, KC // LANES + p)),
            pl.BlockSpec((2048, LANES), lambda b, p, r: (b, VC // LANES + p)),
            cache_spec, cache_spec,
            pl.BlockSpec((None, 4, DA_DIM), lambda b, p, r: (l, 0, 0)),
            pl.BlockSpec((None, 1, LANES), lambda b, p, r: (l, 0, 0)),
            pl.BlockSpec((LANES, LANES), lambda b, p, r: (0, 0)),
        ],
        out_specs=pl.BlockSpec((TQ_DA, LANES), lambda b, p, r: (b * nt + r, p)),
        out_shape=jax.ShapeDtypeStruct((rows, 256), BF16),
        compiler_params=_cparams(("parallel", "parallel", "parallel")),
        name="attn_lat_da",
    )(qkv, qkv, qkv, cache_k, cache_v, da_lambda, subg, bd128)


def _ffn_kernel(x_ref, xp_ref, xn_ref, oa_ref, oap_ref, oan_ref, ob_ref, obp_ref, obn_ref,
                oc_ref, ocp_ref, ocn_ref, mod_ref, g_ref, wo_ref, wup_ref, cw_ref, cb_ref, wd_ref,
                out_ref, x1_sc, h2_sc, gated_sc, ug0_sc, uv0_sc, ug1_sc, uv1_sc, *, seqlen):
    i = pl.program_id(0)
    tm = x_ref.shape[0]

    def prepare():
        gate1 = mod_ref[:, 2 * D_MODEL:3 * D_MODEL]
        shift2 = mod_ref[:, 3 * D_MODEL:4 * D_MODEL]
        scale2 = mod_ref[:, 4 * D_MODEL:5 * D_MODEL]

        def stage(xv, oa, ob, oc):
            mix = _dot(oa, wo_ref[0:256]) + _dot(ob, wo_ref[256:768]) + _dot(oc, wo_ref[768:1024])
            x1 = xv + gate1 * mix
            ms = jnp.mean(x1 * x1, axis=-1, keepdims=True)
            y = x1 * lax.rsqrt(ms + EPS) * g_ref[...]
            return x1, (y * (1.0 + scale2) + shift2).astype(BF16)

        for r0 in range(0, tm, FFN_STAGE_ROWS):
            r1 = r0 + FFN_STAGE_ROWS
            x1, h2 = stage(x_ref[r0:r1], oa_ref[r0:r1], ob_ref[r0:r1], oc_ref[r0:r1])
            x1_sc[r0:r1] = x1
            h2_sc[HALO + r0:HALO + r1] = h2
        h2_sc[0:HALO] = stage(xp_ref[...], oap_ref[...], obp_ref[...], ocp_ref[...])[1]
        h2_sc[HALO + tm:2 * HALO + tm] = stage(xn_ref[...], oan_ref[...], obn_ref[...], ocn_ref[...])[1]

    prepare()
    h2 = h2_sc[...]
    ck = CK_FFN
    mtot = tm + 2 * HALO
    sub = 8
    span = min(seqlen, tm)
    row8 = lax.broadcasted_iota(jnp.int32, (sub, ck), 0)

    def zero_rows(v, group_starts, edge):
        pieces = []
        cur = 0
        for g0 in group_starts:
            if g0 > cur:
                pieces.append(v[cur:g0])
            pos = (i * tm + g0 + row8) & (seqlen - 1)
            pieces.append(jnp.where(pos != edge, v[g0:g0 + sub], 0.0))
            cur = g0 + sub
        if cur < tm:
            pieces.append(v[cur:])
        return jnp.concatenate(pieces, axis=0)

    first_groups = list(range(0, tm, span))
    last_groups = [g + span - sub for g in first_groups]

    def conv(u, col):
        cw = cw_ref[:, col:col + ck]
        up = zero_rows(pltpu.roll(u, 1, 0)[HALO:HALO + tm], first_groups, 0)
        un = zero_rows(pltpu.roll(u, mtot - 1, 0)[HALO:HALO + tm], last_groups, seqlen - 1)
        y = cb_ref[:, col:col + ck] + up * cw[0:1]
        y = y + u[HALO:HALO + tm] * cw[1:2]
        return y + un * cw[2:3]

    def silu_mul(g, v):
        h = 0.5 * g
        return (h * jnp.tanh(h) + h) * v

    nj = D_FF // ck
    bufs = ((ug0_sc, uv0_sc), (ug1_sc, uv1_sc))

    def up_project(c):
        ug_sc, uv_sc = bufs[c % 2]
        ug_sc[...] = _dot(h2, wup_ref[:, c * ck:(c + 1) * ck])
        uv_sc[...] = _dot(h2, wup_ref[:, D_FF + c * ck:D_FF + (c + 1) * ck])

    def gate(c):
        ug_sc, uv_sc = bufs[c % 2]
        yg = conv(ug_sc[...], c * ck)
        yv = conv(uv_sc[...], D_FF + c * ck)
        gated_sc[:, c * ck:(c + 1) * ck] = silu_mul(yg, yv).astype(BF16)

    up_project(0)
    for c in range(nj):
        if c + 1 < nj:
            up_project(c + 1)
        gate(c)
    ffn = _dot(gated_sc[...], wd_ref[...])
    out_ref[...] = x1_sc[...] + mod_ref[:, 5 * D_MODEL:6 * D_MODEL] * ffn


def _ffn(x, oa, ob, oc, mod4, mod_row_fn, l, g_ffn, w_out_bf, w_up_bf, conv_w, conv_b, w_down_bf, seqlen):
    rows = x.shape[0]
    tm = TM_FFN
    nhb = rows // HALO
    per = tm // HALO

    def tile(w):
        return pl.BlockSpec((tm, w), lambda i: (i, 0))

    def prev(w):
        return pl.BlockSpec((HALO, w), lambda i: (jnp.maximum(i * per - 1, 0), 0))

    def nxt(w):
        return pl.BlockSpec((HALO, w), lambda i: (jnp.minimum((i + 1) * per, nhb - 1), 0))

    def trio(w):
        return [tile(w), prev(w), nxt(w)]

    def resident(shape):
        return pl.BlockSpec((None,) + shape, lambda i: (l, 0, 0), pipeline_mode=pl.Buffered(1))

    in_specs = trio(D_MODEL) + trio(256) + trio(512) + trio(256) + [
        pl.BlockSpec((None, None, 1, 6 * D_MODEL), lambda i: (l, mod_row_fn(i, tm), 0, 0)),
        resident((1, D_MODEL)),
        resident((D_MIX, D_MODEL)),
        resident((D_MODEL, 2 * D_FF)),
        resident((3, 2 * D_FF)),
        resident((1, 2 * D_FF)),
        resident((D_FF, D_MODEL)),
    ]
    return pl.pallas_call(
        functools.partial(_ffn_kernel, seqlen=seqlen),
        grid=(rows // tm,),
        in_specs=in_specs,
        out_specs=pl.BlockSpec((tm, D_MODEL), lambda i: (i, 0)),
        out_shape=jax.ShapeDtypeStruct((rows, D_MODEL), F32),
        scratch_shapes=[
            pltpu.VMEM((tm, D_MODEL), F32),
            pltpu.VMEM((tm + 2 * HALO, D_MODEL), BF16),
            pltpu.VMEM((tm, D_FF), BF16),
        ] + [pltpu.VMEM((tm + 2 * HALO, CK_FFN), F32)] * 4,
        compiler_params=_cparams(("parallel",)),
        name="outproj_ffn",
    )(x, x, x, oa, oa, oa, ob, ob, ob, oc, oc, oc, mod4, g_ffn.reshape(DEPTH, 1, D_MODEL),
      w_out_bf, w_up_bf, conv_w, conv_b.reshape(DEPTH, 1, 2 * D_FF), w_down_bf)


def _block_diag_mean(n, group):
    idx = np.arange(n) // group
    return jnp.asarray((idx[:, None] == idx[None, :]).astype(np.float32) / group, dtype=BF16)


def _rope_tables(T, dim):
    n = dim // 4
    inv = 1.0 / (ROPE_BASE ** (jnp.arange(n, dtype=F32) / n))
    t = jnp.arange(T)
    rowp = (t // GRID_W).astype(F32)
    colp = (t % GRID_W).astype(F32)
    ang = jnp.concatenate([rowp[:, None] * inv, colp[:, None] * inv], axis=-1)
    cos = jnp.repeat(jnp.cos(ang), 2, axis=-1)
    sin = jnp.repeat(jnp.sin(ang), 2, axis=-1) * jnp.tile(jnp.asarray([-1.0, 1.0], F32), dim // 2)
    reps = LANES // dim
    return jnp.tile(cos, (1, reps)), jnp.tile(sin, (1, reps))


def _pair_sw_heads(w, axis):
    assert _SW_ORDER == tuple(g * 4 + i for i in range(4) for g in range(2))
    shp = w.shape
    w = w.reshape(shp[:axis] + (2, 4, HEAD_DIM) + shp[axis + 1:])
    return jnp.swapaxes(w, axis, axis + 1).reshape(shp)


def kernel(x_prompt, x_sample, cache_na_k, cache_na_v, cache_sw_k, cache_sw_v, cache_da_k, cache_da_v, c, c_ctx, g_attn, g_ffn, w_mod, b_mod, w_in, na_qk_g, na_rpb, sw_qk_g, sw_sink, da_qk_g, da_lambda, da_subln_g, w_out, w_up, conv_w, conv_b, w_down):
    nb_ctx, seq, _ = x_prompt.shape
    nb_lat, T, _ = x_sample.shape
    npast = cache_na_k.shape[2]

    w_in_bf = jnp.concatenate([
        _pair_sw_heads(w_in[:, :, 768:1280], 2), w_in[:, :, 0:768], w_in[:, :, 1280:]], axis=2).astype(BF16)
    w_out_bf = jnp.concatenate([
        w_out[:, 0:256], _pair_sw_heads(w_out[:, 256:768], 1), w_out[:, 768:]], axis=1).astype(BF16)
    w_up_bf = w_up.astype(BF16)
    w_down_bf = w_down.astype(BF16)

    ones = lambda n: jnp.ones((DEPTH, n), F32)
    sc64 = HEAD_DIM ** -0.5 * LOG2E
    sc32 = DA_DIM ** -0.5 * LOG2E
    gvec = jnp.concatenate([
        jnp.tile(sw_qk_g[:, 0], (1, 8)) * sc64,
        jnp.tile(na_qk_g[:, 0], (1, 4)) * sc64, jnp.tile(na_qk_g[:, 1], (1, 4)), ones(256),
        jnp.tile(sw_qk_g[:, 1], (1, 2)), ones(128),
        jnp.tile(da_qk_g[:, 0], (1, 8)) * sc32, jnp.tile(da_qk_g[:, 1], (1, 8)), ones(256),
    ], axis=-1).reshape(DEPTH, 1, D_IN)
    subg = jnp.tile(da_subln_g, (1, 2)).reshape(DEPTH, 1, LANES)
    bd64 = _block_diag_mean(256, HEAD_DIM)
    bd32 = _block_diag_mean(256, DA_DIM)
    bd128 = _block_diag_mean(LANES, HEAD_DIM)
    rope_tabs = _rope_tables(T, HEAD_DIM) + _rope_tables(T, DA_DIM)

    cna_k = cache_na_k.reshape(nb_lat, DEPTH, npast, 256)
    cna_v = cache_na_v.reshape(nb_lat, DEPTH, npast, 256)
    csw_k = cache_sw_k.reshape(nb_lat, DEPTH, npast, 128)
    csw_v = cache_sw_v.reshape(nb_lat, DEPTH, npast, 128)
    cda_k = cache_da_k.reshape(nb_lat, DEPTH, npast, 256)
    cda_v = cache_da_v.reshape(nb_lat, DEPTH, npast, 256)

    cvecs = jnp.concatenate([c_ctx[None, :], c, jnp.zeros((8 - 1 - nb_lat, D_MODEL), F32)], axis=0)
    mod4 = _modulation(cvecs, w_mod, b_mod).reshape(DEPTH, 8, 1, 6 * D_MODEL)
    bias = _na_bias_tiles(na_rpb)

    ctx_row = lambda i, tm: 0
    lat_row = lambda i, tm: 1 + (i * tm) // T

    xp = x_prompt.reshape(nb_ctx * seq, D_MODEL)
    xs = x_sample.reshape(nb_lat * T, D_MODEL)
    caches = ()
    for l in range(DEPTH):
        lam_init = 0.8 - 0.6 * math.exp(-0.3 * l)
        qkv_p, *caches = _inproj(xp, mod4, ctx_row, l, g_attn, w_in_bf, gvec, bd64, bd32, None, caches, seq)
        oa, ob, oc = _attn_ctx(qkv_p, l, sw_sink, da_lambda, subg, bd128, lam_init)
        xp = _ffn(xp, oa, ob, oc, mod4, ctx_row, l, g_ffn, w_out_bf, w_up_bf, conv_w, conv_b, w_down_bf, seq)
        qkv_s = _inproj(xs, mod4, lat_row, l, g_attn, w_in_bf, gvec, bd64, bd32, rope_tabs, None, T)[0]
        qkv_s3 = qkv_s.reshape(nb_lat, T, D_IN)
        oa = _attn_na(qkv_s3, cna_k, cna_v, bias, l).reshape(nb_lat * T, 256)
        ob = _attn_sw(qkv_s3, csw_k, csw_v, sw_sink, l).reshape(nb_lat * T, 512)
        oc = _attn_da(qkv_s, cda_k, cda_v, da_lambda, subg, bd128, l, lam_init)
        xs = _ffn(xs, oa, ob, oc, mod4, lat_row, l, g_ffn, w_out_bf, w_up_bf, conv_w, conv_b, w_down_bf, T)

    def heads(a, tail):
        return a.reshape((nb_ctx, DEPTH, seq) + tail)

    return (xp.reshape(nb_ctx, seq, D_MODEL), xs.reshape(nb_lat, T, D_MODEL),
            heads(caches[0], (NA_HEADS, HEAD_DIM)), heads(caches[1], (NA_HEADS, HEAD_DIM)),
            heads(caches[2], (SW_KV_HEADS, HEAD_DIM)), heads(caches[3], (SW_KV_HEADS, HEAD_DIM)),
            heads(caches[4], (DA_HEADS, 2, DA_DIM)), heads(caches[5], (DA_HEADS, DA_DIM * 2)))
```

```python
import functools
import math

import numpy as np
import jax
import jax.numpy as jnp
from jax import lax
from jax.experimental import pallas as pl
from jax.experimental.pallas import tpu as pltpu

F32 = jnp.float32
BF16 = jnp.bfloat16

D_MODEL = 1024
DEPTH = 4
GRID_W = 64
HEAD_DIM = 64
NA_HEADS = 4
NA_WIN_ROWS = 8
NA_WIN_COLS = 16
SW_HEADS = 8
SW_KV_HEADS = 2
SW_WINDOW = 128
DA_HEADS = 4
DA_DIM = 32
D_FF = 2816
ROPE_BASE = 10000.0
EPS = 1e-6

QB = 0
QA, KA, VA = 512, 768, 1024
KB, VB = 1280, 1408
QC, KC, VC = 1536, 1792, 2048
D_IN = 2304
D_MIX = 1024
_SW_ORDER = (0, 4, 1, 5, 2, 6, 3, 7)

LANES = 128
NEG = -1e30
LOG2E = math.log2(math.e)
TQ = 256
TQ_DA = 1024
CTX_SEQS = 4
LAT_BATCHES = 4
TM_IN = 1024
TM_FFN = 512
HALO = 16
CK_FFN = 256
FFN_STAGE_ROWS = 256
VMEM_LIMIT = 56 * 1024 * 1024


def _cparams(sem):
    return pltpu.CompilerParams(dimension_semantics=sem, vmem_limit_bytes=VMEM_LIMIT)


def _dot(a, b):
    return jnp.dot(a, b, preferred_element_type=F32)


def _dot_nt(a, b):
    return lax.dot_general(a, b, (((1,), (1,)), ((), ())), preferred_element_type=F32)


def _sigmoid(x):
    return 1.0 / (1.0 + jnp.exp(-x))


def _mod_kernel(c_ref, w_ref, b_ref, o_ref):
    c = c_ref[...]
    s = (c * _sigmoid(c)).astype(BF16)
    o_ref[...] = _dot(s, w_ref[...].astype(BF16)) + b_ref[...]


def _modulation(cvecs, w_mod, b_mod):
    nchunk = 4
    cn = 6 * D_MODEL // nchunk
    return pl.pallas_call(
        _mod_kernel,
        grid=(DEPTH, nchunk),
        in_specs=[
            pl.BlockSpec((8, D_MODEL), lambda l, n: (0, 0)),
            pl.BlockSpec((None, D_MODEL, cn), lambda l, n: (l, 0, n)),
            pl.BlockSpec((None, 1, cn), lambda l, n: (l, 0, n)),
        ],
        out_specs=pl.BlockSpec((None, 8, cn), lambda l, n: (l, 0, n)),
        out_shape=jax.ShapeDtypeStruct((DEPTH, 8, 6 * D_MODEL), F32),
        compiler_params=_cparams(("arbitrary", "arbitrary")),
        name="modulation",
    )(cvecs, w_mod, b_mod.reshape(DEPTH, 1, 6 * D_MODEL))


N_RPB_COLS = 2 * NA_WIN_COLS - 1
_BIAS_KINDS = (
    (3, lambda i, j: j >= i),
    (7, lambda i, j: True),
    (11, lambda i, j: j < i),
    (3, lambda i, j: True),
    (11, lambda i, j: True),
    (0, lambda i, j: False),
)
K_PREV, K_CUR, K_NEXT, K_PREV_FULL, K_NEXT_FULL, K_NONE = range(6)


def _bias_kernel(rpb_ref, out_ref):
    lh = pl.program_id(0)
    qcol = lax.broadcasted_iota(jnp.int32, (GRID_W, LANES), 0)
    lane = lax.broadcasted_iota(jnp.int32, (GRID_W, LANES), 1)
    kcol = lane & (GRID_W - 1)
    hi = lane >= GRID_W
    f = jnp.clip(kcol - qcol, -(NA_WIN_COLS - 1), NA_WIN_COLS - 1) + (NA_WIN_COLS - 1)
    qs = jnp.clip(qcol - NA_WIN_COLS // 2, 0, GRID_W - NA_WIN_COLS)
    cvalid = (kcol >= qs) & (kcol < qs + NA_WIN_COLS)
    masked = jnp.full((GRID_W, LANES), NEG, F32)

    @functools.lru_cache(maxsize=None)
    def block_pair(a0, a1):
        if a0 is None and a1 is None:
            return masked
        acc = jnp.zeros((GRID_W, LANES), F32)
        for b in range(N_RPB_COLS):
            s0 = rpb_ref[lh, a0 * N_RPB_COLS + b] if a0 is not None else 0.0
            s1 = rpb_ref[lh, a1 * N_RPB_COLS + b] if a1 is not None else 0.0
            acc = jnp.where(f == b, jnp.where(hi, s1, s0), acc)
        valid = cvalid
        if a0 is None:
            valid = valid & hi
        if a1 is None:
            valid = valid & jnp.logical_not(hi)
        return jnp.where(valid, acc * LOG2E, masked)

    for kind, (off, inside) in enumerate(_BIAS_KINDS):
        for i in range(4):
            for jt in range(2):
                a = [(j - i + off) if inside(i, j) else None for j in (2 * jt, 2 * jt + 1)]
                out_ref[kind, i * GRID_W:(i + 1) * GRID_W, jt * LANES:(jt + 1) * LANES] = block_pair(a[0], a[1])


def _na_bias_tiles(na_rpb):
    n = DEPTH * NA_HEADS
    rpb2 = na_rpb.reshape(n, (2 * NA_WIN_ROWS - 1) * N_RPB_COLS)
    out = pl.pallas_call(
        _bias_kernel,
        grid=(n,),
        in_specs=[pl.BlockSpec(memory_space=pltpu.SMEM)],
        out_specs=pl.BlockSpec((None, len(_BIAS_KINDS), TQ, TQ), lambda i: (i, 0, 0, 0)),
        out_shape=jax.ShapeDtypeStruct((n, len(_BIAS_KINDS), TQ, TQ), F32),
        compiler_params=_cparams(("arbitrary",)),
        name="na_bias_tiles",
    )(rpb2)
    return out.reshape(DEPTH, NA_HEADS, len(_BIAS_KINDS), TQ, TQ)


_NORM64 = ((QB, 256), (QB + 256, 256), (QA, 256), (KA, 256), (KB, 128))
_NORM32 = ((QC, 256), (KC, 256))
_PLAIN = ((VA, 256), (VB, 128), (VC, 256))
_ROPE_B = (QB, QB + 256, KB)
_ROPE_C = (QC, KC)
_CACHE_COLS = (KA, VA, KB, VB, KC, VC)


def _inproj_kernel(*refs, rope, n_alias):
    x_ref, mod_ref, g_ref, w_ref, gvec_ref, bd64_ref, bd32_ref = refs[:7]
    rest = refs[7:]
    if rope:
        cosb_ref, sinb_ref, cosc_ref, sinc_ref = rest[:4]
        rest = rest[4:]
    rest = rest[n_alias:]
    qkv_ref = rest[0]
    cache_refs = rest[1:]

    x = x_ref[...]
    ms = jnp.mean(x * x, axis=-1, keepdims=True)
    y = x * lax.rsqrt(ms + EPS) * g_ref[...]
    h = (y * (1.0 + mod_ref[:, D_MODEL:2 * D_MODEL]) + mod_ref[:, 0:D_MODEL]).astype(BF16)
    tm = x.shape[0]
    even = (lax.broadcasted_iota(jnp.int32, (tm, LANES), 1) & 1) == 0

    def rotary(t, cos_ref, sin_ref):
        cos = cos_ref[...]
        sin = sin_ref[...]
        parts = []
        for c in range(t.shape[1] // LANES):
            tc = t[:, c * LANES:(c + 1) * LANES]
            partner = jnp.where(even, pltpu.roll(tc, LANES - 1, 1), pltpu.roll(tc, 1, 1))
            parts.append(tc * cos + partner * sin)
        return parts[0] if len(parts) == 1 else jnp.concatenate(parts, axis=1)

    cache_of = dict(zip(_CACHE_COLS, cache_refs))

    def emit(a, w, t):
        qkv_ref[:, a:a + w] = t.astype(BF16)
        if a in cache_of:
            ref = cache_of[a]
            seq = ref.shape[1]
            for n in range(ref.shape[0]):
                ref[n] = t[n * seq:(n + 1) * seq]

    p = _dot(h, w_ref[...])
    for segs, bd_ref in ((_NORM64, bd64_ref), (_NORM32, bd32_ref)):
        for a, w in segs:
            t = p[:, a:a + w]
            msq = _dot((t * t).astype(BF16), bd_ref[0:w, 0:w])
            t = t * lax.rsqrt(msq + EPS) * gvec_ref[:, a:a + w]
            if rope and a in _ROPE_B:
                t = rotary(t, cosb_ref, sinb_ref)
            if rope and a in _ROPE_C:
                t = rotary(t, cosc_ref, sinc_ref)
            emit(a, w, t)
    for a, w in _PLAIN:
        emit(a, w, p[:, a:a + w])


def _inproj(x, mod4, mod_row_fn, l, g_attn, w_in_bf, gvec, bd64, bd32, rope_tabs, caches, seq):
    rows = x.shape[0]
    tm = TM_IN
    grid = (rows // tm,)
    rope = rope_tabs is not None
    in_specs = [
        pl.BlockSpec((tm, D_MODEL), lambda i: (i, 0)),
        pl.BlockSpec((None, None, 1, 6 * D_MODEL), lambda i: (l, mod_row_fn(i, tm), 0, 0)),
        pl.BlockSpec((None, 1, D_MODEL), lambda i: (l, 0, 0)),
        pl.BlockSpec((None, D_MODEL, D_IN), lambda i: (l, 0, 0), pipeline_mode=pl.Buffered(1)),
        pl.BlockSpec((None, 1, D_IN), lambda i: (l, 0, 0)),
        pl.BlockSpec((256, 256), lambda i: (0, 0)),
        pl.BlockSpec((256, 256), lambda i: (0, 0)),
    ]
    args = [x, mod4, g_attn.reshape(DEPTH, 1, D_MODEL), w_in_bf, gvec, bd64, bd32]
    if rope:
        nt = rope_tabs[0].shape[0] // tm
        in_specs += [pl.BlockSpec((tm, LANES), lambda i: (i % nt, 0))] * 4
        args += list(rope_tabs)
    out_shape = [jax.ShapeDtypeStruct((rows, D_IN), BF16)]
    out_specs = [pl.BlockSpec((tm, D_IN), lambda i: (i, 0))]
    aliases = {}
    n_alias = 0
    if caches is not None:
        per = tm // seq
        n_alias = len(caches)
        for n, w in enumerate((256, 256, 128, 128, 256, 256)):
            out_shape.append(jax.ShapeDtypeStruct((rows // seq, DEPTH, seq, w), F32))
            out_specs.append(pl.BlockSpec((per, None, seq, w), lambda i: (i, l, 0, 0)))
            if caches:
                aliases[len(args) + n] = 1 + n
        in_specs += [pl.BlockSpec(memory_space=pl.ANY)] * n_alias
        args += list(caches)
    return pl.pallas_call(
        functools.partial(_inproj_kernel, rope=rope, n_alias=n_alias),
        grid=grid,
        in_specs=in_specs,
        out_specs=out_specs,
        out_shape=out_shape,
        input_output_aliases=aliases,
        compiler_params=_cparams(("parallel",)),
        name="inproj_lat" if rope else "inproj_ctx",
    )(*args)


def _lane_masks():
    lane = lax.broadcasted_iota(jnp.int32, (1, LANES), 1)
    lo = lane < HEAD_DIM
    quarters = [(lane >= DA_DIM * j) & (lane < DA_DIM * (j + 1)) for j in range(4)]
    return lo, quarters


def _keep(q, mask):
    return jnp.where(mask, q, jnp.zeros_like(q))


def _attend(qm, ks, vs, biases=None, sink=None):
    if sink is not None:
        sink = sink * LOG2E
    ss = []
    for n, k in enumerate(ks):
        s = _dot_nt(qm, k)
        if biases is not None and biases[n] is not None:
            s = s + biases[n]
        ss.append(s)
    m = ss[0].max(axis=-1, keepdims=True)
    for s in ss[1:]:
        m = jnp.maximum(m, s.max(axis=-1, keepdims=True))
    if sink is not None:
        m = jnp.maximum(m, sink)
    l = None
    o = None
    for s, v in zip(ss, vs):
        p = jnp.exp2(s - m)
        ls = p.sum(axis=-1, keepdims=True)
        os = _dot(p.astype(BF16), v)
        l = ls if l is None else l + ls
        o = os if o is None else o + os
    if sink is not None:
        l = l + jnp.exp2(sink - m)
    return o * (1.0 / l)


def _diff_lambda(lam_ref, lam_init):
    lp = lam_ref[...]
    a = jnp.sum(lp[0:1] * lp[1:2], axis=-1, keepdims=True)
    b = jnp.sum(lp[2:3] * lp[3:4], axis=-1, keepdims=True)
    return jnp.exp(a) - jnp.exp(b) + lam_init


def _diff_pair(q, ks, vs, lam, subg, bd, lam_init, lo, quarters):
    o = [_attend(_keep(q, quarters[j]), ks, vs) for j in range(4)]
    d = jnp.where(lo, o[0] - lam * o[1], o[2] - lam * o[3])
    msq = _dot((d * d).astype(BF16), bd)
    return d * lax.rsqrt(msq + EPS) * subg * (1.0 - lam_init)


def _attn_ctx_kernel(qkv_ref, sink_ref, lam_ref, subg_ref, bd_ref, oa_ref, ob_ref, oc_ref, *, lam_init):
    lo, quarters = _lane_masks()
    hi = jnp.logical_not(lo)
    lam = _diff_lambda(lam_ref, lam_init)
    def one_sequence(n, carry):
        rows = pl.ds(pl.multiple_of(n * TQ, TQ), TQ)
        for pr in range(2):
            c = pr * LANES
            q = qkv_ref[rows, QA + c:QA + c + LANES]
            k = [qkv_ref[rows, KA + c:KA + c + LANES]]
            v = [qkv_ref[rows, VA + c:VA + c + LANES]]
            o = jnp.where(lo, _attend(_keep(q, lo), k, v), _attend(_keep(q, hi), k, v))
            oa_ref[rows, c:c + LANES] = o.astype(BF16)
        k = [qkv_ref[rows, KB:KB + LANES]]
        v = [qkv_ref[rows, VB:VB + LANES]]
        for pr in range(4):
            c = pr * LANES
            q = qkv_ref[rows, QB + c:QB + c + LANES]
            o0 = _attend(_keep(q, lo), k, v, sink=sink_ref[_SW_ORDER[2 * pr]])
            o1 = _attend(_keep(q, hi), k, v, sink=sink_ref[_SW_ORDER[2 * pr + 1]])
            ob_ref[rows, c:c + LANES] = jnp.where(lo, o0, o1).astype(BF16)
        for pr in range(2):
            c = pr * LANES
            q = qkv_ref[rows, QC + c:QC + c + LANES]
            k = [qkv_ref[rows, KC + c:KC + c + LANES]]
            v = [qkv_ref[rows, VC + c:VC + c + LANES]]
            o = _diff_pair(q, k, v, lam, subg_ref[...], bd_ref[...], lam_init, lo, quarters)
            oc_ref[rows, c:c + LANES] = o.astype(BF16)
        return carry

    lax.fori_loop(0, qkv_ref.shape[0] // TQ, one_sequence, 0)


def _attn_ctx(qkv, l, sw_sink, da_lambda, subg, bd128, lam_init):
    rows = qkv.shape[0]
    tr = CTX_SEQS * TQ
    return pl.pallas_call(
        functools.partial(_attn_ctx_kernel, lam_init=lam_init),
        grid=(rows // tr,),
        in_specs=[
            pl.BlockSpec((tr, D_IN), lambda b: (b, 0)),
            pl.BlockSpec(memory_space=pltpu.SMEM),
            pl.BlockSpec((None, 4, DA_DIM), lambda b: (l, 0, 0)),
            pl.BlockSpec((None, 1, LANES), lambda b: (l, 0, 0)),
            pl.BlockSpec((LANES, LANES), lambda b: (0, 0)),
        ],
        out_specs=[
            pl.BlockSpec((tr, 256), lambda b: (b, 0)),
            pl.BlockSpec((tr, 512), lambda b: (b, 0)),
            pl.BlockSpec((tr, 256), lambda b: (b, 0)),
        ],
        out_shape=[
            jax.ShapeDtypeStruct((rows, 256), BF16),
            jax.ShapeDtypeStruct((rows, 512), BF16),
            jax.ShapeDtypeStruct((rows, 256), BF16),
        ],
        compiler_params=_cparams(("parallel",)),
        name="attn_ctx",
    )(qkv, sw_sink[l], da_lambda, subg, bd128)


NT_LAT = 2048 // TQ


def _attn_na_kernel(q_ref, kp_ref, kc_ref, kn_ref, vp_ref, vc_ref, vn_ref, ck_ref, cv_ref,
                    bp_ref, bc_ref, bn_ref, o_ref):
    lo, _ = _lane_masks()
    hi = jnp.logical_not(lo)
    def one_batch(n, carry):
        ck = ck_ref[n].astype(BF16)
        cv = cv_ref[n].astype(BF16)
        for pr in range(2):
            c = pr * LANES
            sl = slice(c, c + LANES)
            q = q_ref[n, :, sl]
            ks = [kp_ref[n, :, sl], kc_ref[n, :, sl], kn_ref[n, :, sl], ck[:, sl]]
            vs = [vp_ref[n, :, sl], vc_ref[n, :, sl], vn_ref[n, :, sl], cv[:, sl]]
            outs = []
            for hl, mask in enumerate((lo, hi)):
                h = 2 * pr + hl
                outs.append(_attend(_keep(q, mask), ks, vs, biases=[bp_ref[h], bc_ref[h], bn_ref[h], None]))
            o_ref[n, :, sl] = jnp.where(lo, outs[0], outs[1]).astype(BF16)
        return carry

    lax.fori_loop(0, q_ref.shape[0], one_batch, 0)


def _attn_na(qkv, cache_k, cache_v, bias, l):
    nbatch, _, _ = qkv.shape
    nb = LAT_BATCHES
    last = NT_LAT - 1

    def kv_specs(col):
        return [
            pl.BlockSpec((nb, TQ, 256), lambda r, b: (b, jnp.maximum(r - 1, 0), col)),
            pl.BlockSpec((nb, TQ, 256), lambda r, b: (b, r, col)),
            pl.BlockSpec((nb, TQ, 256), lambda r, b: (b, jnp.minimum(r + 1, last), col)),
        ]

    def kind_prev(r):
        return jnp.where(r == 0, K_NONE, jnp.where(r == last, K_PREV_FULL, K_PREV))

    def kind_next(r):
        return jnp.where(r == 0, K_NEXT_FULL, jnp.where(r == last, K_NONE, K_NEXT))

    def bias_spec(kind_fn):
        return pl.BlockSpec((None, NA_HEADS, None, TQ, TQ), lambda r, b: (l, 0, kind_fn(r), 0, 0))

    cache_spec = pl.BlockSpec((nb, None, 512, 256), lambda r, b: (b, l, 0, 0))
    return pl.pallas_call(
        _attn_na_kernel,
        grid=(NT_LAT, nbatch // nb),
        in_specs=[pl.BlockSpec((nb, TQ, 256), lambda r, b: (b, r, QA // 256))]
        + kv_specs(KA // 256) + kv_specs(VA // 256)
        + [cache_spec, cache_spec,
           bias_spec(kind_prev), bias_spec(lambda r: K_CUR), bias_spec(kind_next)],
        out_specs=pl.BlockSpec((nb, TQ, 256), lambda r, b: (b, r, 0)),
        out_shape=jax.ShapeDtypeStruct((nbatch, NT_LAT * TQ, 256), BF16),
        compiler_params=_cparams(("parallel", "parallel")),
        name="attn_lat_na",
    )(qkv, qkv, qkv, qkv, qkv, qkv, qkv, cache_k, cache_v, bias, bias, bias)


def _attn_sw_kernel(q_ref, k0_ref, k1_ref, k2_ref, k3_ref, v0_ref, v1_ref, v2_ref, v3_ref,
                    ck_ref, cv_ref, sink_ref, o_ref):
    r = pl.program_id(1)
    lo, _ = _lane_masks()
    hi = jnp.logical_not(lo)
    qi = lax.broadcasted_iota(jnp.int32, (TQ, TQ), 0)
    kj = lax.broadcasted_iota(jnp.int32, (TQ, TQ), 1)
    zero = jnp.zeros((TQ, TQ), F32)
    neg = jnp.full((TQ, TQ), NEG, F32)
    bias_a = jnp.where((kj >= qi) & ((r > 0) | (kj >= SW_WINDOW)), zero, neg)
    bias_b = jnp.where((kj <= qi) & ((r < NT_LAT - 1) | (kj < SW_WINDOW)), zero, neg)
    biases = [bias_a, bias_b, None]
    def one_batch(n, carry):
        ks = [jnp.concatenate([k0_ref[n], k1_ref[n]], axis=0),
              jnp.concatenate([k2_ref[n], k3_ref[n]], axis=0), ck_ref[n].astype(BF16)]
        vs = [jnp.concatenate([v0_ref[n], v1_ref[n]], axis=0),
              jnp.concatenate([v2_ref[n], v3_ref[n]], axis=0), cv_ref[n].astype(BF16)]
        for pr in range(4):
            c = pr * LANES
            q = q_ref[n, :, c:c + LANES]
            o0 = _attend(_keep(q, lo), ks, vs, biases=biases, sink=sink_ref[_SW_ORDER[2 * pr]])
            o1 = _attend(_keep(q, hi), ks, vs, biases=biases, sink=sink_ref[_SW_ORDER[2 * pr + 1]])
            o_ref[n, :, c:c + LANES] = jnp.where(lo, o0, o1).astype(BF16)
        return carry

    lax.fori_loop(0, q_ref.shape[0], one_batch, 0)


def _attn_sw(qkv, cache_k, cache_v, sw_sink, l):
    nbatch, T, _ = qkv.shape
    nb = LAT_BATCHES
    nblk = T // SW_WINDOW

    def kv_specs(col):
        return [pl.BlockSpec((nb, SW_WINDOW, LANES),
                             lambda b, r, t=t: (b, jnp.clip(2 * r - 1 + t, 0, nblk - 1), col))
                for t in range(4)]

    cache_spec = pl.BlockSpec((nb, None, 512, LANES), lambda b, r: (b, l, 0, 0))
    return pl.pallas_call(
        _attn_sw_kernel,
        grid=(nbatch // nb, NT_LAT),
        in_specs=[pl.BlockSpec((nb, TQ, 512), lambda b, r: (b, r, QB // 512))]
        + kv_specs(KB // LANES) + kv_specs(VB // LANES)
        + [cache_spec, cache_spec, pl.BlockSpec(memory_space=pltpu.SMEM)],
        out_specs=pl.BlockSpec((nb, TQ, 512), lambda b, r: (b, r, 0)),
        out_shape=jax.ShapeDtypeStruct((nbatch, T, 512), BF16),
        compiler_params=_cparams(("parallel", "parallel")),
        name="attn_lat_sw",
    )(*([qkv] * 9), cache_k, cache_v, sw_sink[l])


def _attn_da_kernel(q_ref, k_ref, v_ref, ck_ref, cv_ref, lam_ref, subg_ref, bd_ref, o_ref, *, lam_init):
    lo, quarters = _lane_masks()
    lam = _diff_lambda(lam_ref, lam_init)
    ks = [k_ref[...], ck_ref[...].astype(BF16)]
    vs = [v_ref[...], cv_ref[...].astype(BF16)]
    o = _diff_pair(q_ref[...], ks, vs, lam, subg_ref[...], bd_ref[...], lam_init, lo, quarters)
    o_ref[...] = o.astype(BF16)


def _attn_da(qkv, cache_k, cache_v, da_lambda, subg, bd128, l, lam_init):
    rows = qkv.shape[0]
    nbatch = rows // 2048
    cache_spec = pl.BlockSpec((None, None, 512, LANES), lambda b, p, r: (b, l, 0, p))
    nt = 2048 // TQ_DA
    return pl.pallas_call(
        functools.partial(_attn_da_kernel, lam_init=lam_init),
        grid=(nbatch, 2, nt),
        in_specs=[
            pl.BlockSpec((TQ_DA, LANES), lambda b, p, r: (b * nt + r, QC // LANES + p)),
            pl.BlockSpec((2048, LANES), lambda b, p, r: (b, KC // LANES + p)),
            pl.BlockSpec((2048, LANES), lambda b, p, r: (b, VC // LANES + p)),
            cache_spec, cache_spec,
            pl.BlockSpec((None, 4, DA_DIM), lambda b, p, r: (l, 0, 0)),
            pl.BlockSpec((None, 1, LANES), lambda b, p, r: (l, 0, 0)),
            pl.BlockSpec((LANES, LANES), lambda b, p, r: (0, 0)),
        ],
        out_specs=pl.BlockSpec((TQ_DA, LANES), lambda b, p, r: (b * nt + r, p)),
        out_shape=jax.ShapeDtypeStruct((rows, 256), BF16),
        compiler_params=_cparams(("parallel", "parallel", "parallel")),
        name="attn_lat_da",
    )(qkv, qkv, qkv, cache_k, cache_v, da_lambda, subg, bd128)


def _ffn_kernel(x_ref, xp_ref, xn_ref, oa_ref, oap_ref, oan_ref, ob_ref, obp_ref, obn_ref,
                oc_ref, ocp_ref, ocn_ref, mod_ref, g_ref, wo_ref, wup_ref, cw_ref, cb_ref, wd_ref,
                out_ref, x1_sc, h2_sc, gated_sc, ug0_sc, uv0_sc, ug1_sc, uv1_sc, *, seqlen):
    i = pl.program_id(0)
    tm = x_ref.shape[0]

    def prepare():
        gate1 = mod_ref[:, 2 * D_MODEL:3 * D_MODEL]
        shift2 = mod_ref[:, 3 * D_MODEL:4 * D_MODEL]
        scale2 = mod_ref[:, 4 * D_MODEL:5 * D_MODEL]

        def stage(xv, oa, ob, oc):
            mix = _dot(oa, wo_ref[0:256]) + _dot(ob, wo_ref[256:768]) + _dot(oc, wo_ref[768:1024])
            x1 = xv + gate1 * mix
            ms = jnp.mean(x1 * x1, axis=-1, keepdims=True)
            y = x1 * lax.rsqrt(ms + EPS) * g_ref[...]
            return x1, (y * (1.0 + scale2) + shift2).astype(BF16)

        for r0 in range(0, tm, FFN_STAGE_ROWS):
            r1 = r0 + FFN_STAGE_ROWS
            x1, h2 = stage(x_ref[r0:r1], oa_ref[r0:r1], ob_ref[r0:r1], oc_ref[r0:r1])
            x1_sc[r0:r1] = x1
            h2_sc[HALO + r0:HALO + r1] = h2
        h2_sc[0:HALO] = stage(xp_ref[...], oap_ref[...], obp_ref[...], ocp_ref[...])[1]
        h2_sc[HALO + tm:2 * HALO + tm] = stage(xn_ref[...], oan_ref[...], obn_ref[...], ocn_ref[...])[1]

    prepare()
    h2 = h2_sc[...]
    ck = CK_FFN
    mtot = tm + 2 * HALO
    sub = 8
    span = min(seqlen, tm)
    row8 = lax.broadcasted_iota(jnp.int32, (sub, ck), 0)

    def zero_rows(v, group_starts, edge):
        pieces = []
        cur = 0
        for g0 in group_starts:
            if g0 > cur:
                pieces.append(v[cur:g0])
            pos = (i * tm + g0 + row8) & (seqlen - 1)
            pieces.append(jnp.where(pos != edge, v[g0:g0 + sub], 0.0))
            cur = g0 + sub
        if cur < tm:
            pieces.append(v[cur:])
        return jnp.concatenate(pieces, axis=0)

    first_groups = list(range(0, tm, span))
    last_groups = [g + span - sub for g in first_groups]

    def conv(u, col):
        cw = cw_ref[:, col:col + ck]
        up = zero_rows(pltpu.roll(u, 1, 0)[HALO:HALO + tm], first_groups, 0)
        un = zero_rows(pltpu.roll(u, mtot - 1, 0)[HALO:HALO + tm], last_groups, seqlen - 1)
        y = cb_ref[:, col:col + ck] + up * cw[0:1]
        y = y + u[HALO:HALO + tm] * cw[1:2]
        return y + un * cw[2:3]

    def silu_mul(g, v):
        h = 0.5 * g
        return (h * jnp.tanh(h) + h) * v

    nj = D_FF // ck
    bufs = ((ug0_sc, uv0_sc), (ug1_sc, uv1_sc))

    def up_project(c):
        ug_sc, uv_sc = bufs[c % 2]
        ug_sc[...] = _dot(h2, wup_ref[:, c * ck:(c + 1) * ck])
        uv_sc[...] = _dot(h2, wup_ref[:, D_FF + c * ck:D_FF + (c + 1) * ck])

    def gate(c):
        ug_sc, uv_sc = bufs[c % 2]
        yg = conv(ug_sc[...], c * ck)
        yv = conv(uv_sc[...], D_FF + c * ck)
        gated_sc[:, c * ck:(c + 1) * ck] = silu_mul(yg, yv).astype(BF16)

    up_project(0)
    for c in range(nj):
        if c + 1 < nj:
            up_project(c + 1)
        gate(c)
    ffn = _dot(gated_sc[...], wd_ref[...])
    out_ref[...] = x1_sc[...] + mod_ref[:, 5 * D_MODEL:6 * D_MODEL] * ffn


def _ffn(x, oa, ob, oc, mod4, mod_row_fn, l, g_ffn, w_out_bf, w_up_bf, conv_w, conv_b, w_down_bf, seqlen):
    rows = x.shape[0]
    tm = TM_FFN
    nhb = rows // HALO
    per = tm // HALO

    def tile(w):
        return pl.BlockSpec((tm, w), lambda i: (i, 0))

    def prev(w):
        return pl.BlockSpec((HALO, w), lambda i: (jnp.maximum(i * per - 1, 0), 0))

    def nxt(w):
        return pl.BlockSpec((HALO, w), lambda i: (jnp.minimum((i + 1) * per, nhb - 1), 0))

    def trio(w):
        return [tile(w), prev(w), nxt(w)]

    def resident(shape):
        return pl.BlockSpec((None,) + shape, lambda i: (l, 0, 0), pipeline_mode=pl.Buffered(1))

    in_specs = trio(D_MODEL) + trio(256) + trio(512) + trio(256) + [
        pl.BlockSpec((None, None, 1, 6 * D_MODEL), lambda i: (l, mod_row_fn(i, tm), 0, 0)),
        resident((1, D_MODEL)),
        resident((D_MIX, D_MODEL)),
        resident((D_MODEL, 2 * D_FF)),
        resident((3, 2 * D_FF)),
        resident((1, 2 * D_FF)),
        resident((D_FF, D_MODEL)),
    ]
    return pl.pallas_call(
        functools.partial(_ffn_kernel, seqlen=seqlen),
        grid=(rows // tm,),
        in_specs=in_specs,
        out_specs=pl.BlockSpec((tm, D_MODEL), lambda i: (i, 0)),
        out_shape=jax.ShapeDtypeStruct((rows, D_MODEL), F32),
        scratch_shapes=[
            pltpu.VMEM((tm, D_MODEL), F32),
            pltpu.VMEM((tm + 2 * HALO, D_MODEL), BF16),
            pltpu.VMEM((tm, D_FF), BF16),
        ] + [pltpu.VMEM((tm + 2 * HALO, CK_FFN), F32)] * 4,
        compiler_params=_cparams(("parallel",)),
        name="outproj_ffn",
    )(x, x, x, oa, oa, oa, ob, ob, ob, oc, oc, oc, mod4, g_ffn.reshape(DEPTH, 1, D_MODEL),
      w_out_bf, w_up_bf, conv_w, conv_b.reshape(DEPTH, 1, 2 * D_FF), w_down_bf)


def _block_diag_mean(n, group):
    idx = np.arange(n) // group
    return jnp.asarray((idx[:, None] == idx[None, :]).astype(np.float32) / group, dtype=BF16)


def _rope_tables(T, dim):
    n = dim // 4
    inv = 1.0 / (ROPE_BASE ** (jnp.arange(n, dtype=F32) / n))
    t = jnp.arange(T)
    rowp = (t // GRID_W).astype(F32)
    colp = (t % GRID_W).astype(F32)
    ang = jnp.concatenate([rowp[:, None] * inv, colp[:, None] * inv], axis=-1)
    cos = jnp.repeat(jnp.cos(ang), 2, axis=-1)
    sin = jnp.repeat(jnp.sin(ang), 2, axis=-1) * jnp.tile(jnp.asarray([-1.0, 1.0], F32), dim // 2)
    reps = LANES // dim
    return jnp.tile(cos, (1, reps)), jnp.tile(sin, (1, reps))


def _pair_sw_heads(w, axis):
    assert _SW_ORDER == tuple(g * 4 + i for i in range(4) for g in range(2))
    shp = w.shape
    w = w.reshape(shp[:axis] + (2, 4, HEAD_DIM) + shp[axis + 1:])
    return jnp.swapaxes(w, axis, axis + 1).reshape(shp)


def kernel(x_prompt, x_sample, cache_na_k, cache_na_v, cache_sw_k, cache_sw_v, cache_da_k, cache_da_v, c, c_ctx, g_attn, g_ffn, w_mod, b_mod, w_in, na_qk_g, na_rpb, sw_qk_g, sw_sink, da_qk_g, da_lambda, da_subln_g, w_out, w_up, conv_w, conv_b, w_down):
    nb_ctx, seq, _ = x_prompt.shape
    nb_lat, T, _ = x_sample.shape
    npast = cache_na_k.shape[2]

    w_in_bf = jnp.concatenate([
        _pair_sw_heads(w_in[:, :, 768:1280], 2), w_in[:, :, 0:768], w_in[:, :, 1280:]], axis=2).astype(BF16)
    w_out_bf = jnp.concatenate([
        w_out[:, 0:256], _pair_sw_heads(w_out[:, 256:768], 1), w_out[:, 768:]], axis=1).astype(BF16)
    w_up_bf = w_up.astype(BF16)
    w_down_bf = w_down.astype(BF16)

    ones = lambda n: jnp.ones((DEPTH, n), F32)
    sc64 = HEAD_DIM ** -0.5 * LOG2E
    sc32 = DA_DIM ** -0.5 * LOG2E
    gvec = jnp.concatenate([
        jnp.tile(sw_qk_g[:, 0], (1, 8)) * sc64,
        jnp.tile(na_qk_g[:, 0], (1, 4)) * sc64, jnp.tile(na_qk_g[:, 1], (1, 4)), ones(256),
        jnp.tile(sw_qk_g[:, 1], (1, 2)), ones(128),
        jnp.tile(da_qk_g[:, 0], (1, 8)) * sc32, jnp.tile(da_qk_g[:, 1], (1, 8)), ones(256),
    ], axis=-1).reshape(DEPTH, 1, D_IN)
    subg = jnp.tile(da_subln_g, (1, 2)).reshape(DEPTH, 1, LANES)
    bd64 = _block_diag_mean(256, HEAD_DIM)
    bd32 = _block_diag_mean(256, DA_DIM)
    bd128 = _block_diag_mean(LANES, HEAD_DIM)
    rope_tabs = _rope_tables(T, HEAD_DIM) + _rope_tables(T, DA_DIM)

    cna_k = cache_na_k.reshape(nb_lat, DEPTH, npast, 256)
    cna_v = cache_na_v.reshape(nb_lat, DEPTH, npast, 256)
    csw_k = cache_sw_k.reshape(nb_lat, DEPTH, npast, 128)
    csw_v = cache_sw_v.reshape(nb_lat, DEPTH, npast, 128)
    cda_k = cache_da_k.reshape(nb_lat, DEPTH, npast, 256)
    cda_v = cache_da_v.reshape(nb_lat, DEPTH, npast, 256)

    cvecs = jnp.concatenate([c_ctx[None, :], c, jnp.zeros((8 - 1 - nb_lat, D_MODEL), F32)], axis=0)
    mod4 = _modulation(cvecs, w_mod, b_mod).reshape(DEPTH, 8, 1, 6 * D_MODEL)
    bias = _na_bias_tiles(na_rpb)

    ctx_row = lambda i, tm: 0
    lat_row = lambda i, tm: 1 + (i * tm) // T

    xp = x_prompt.reshape(nb_ctx * seq, D_MODEL)
    xs = x_sample.reshape(nb_lat * T, D_MODEL)
    caches = ()
    for l in range(DEPTH):
        lam_init = 0.8 - 0.6 * math.exp(-0.3 * l)
        qkv_p, *caches = _inproj(xp, mod4, ctx_row, l, g_attn, w_in_bf, gvec, bd64, bd32, None, caches, seq)
        oa, ob, oc = _attn_ctx(qkv_p, l, sw_sink, da_lambda, subg, bd128, lam_init)
        xp = _ffn(xp, oa, ob, oc, mod4, ctx_row, l, g_ffn, w_out_bf, w_up_bf, conv_w, conv_b, w_down_bf, seq)
        qkv_s = _inproj(xs, mod4, lat_row, l, g_attn, w_in_bf, gvec, bd64, bd32, rope_tabs, None, T)[0]
        qkv_s3 = qkv_s.reshape(nb_lat, T, D_IN)
        oa = _attn_na(qkv_s3, cna_k, cna_v, bias, l).reshape(nb_lat * T, 256)
        ob = _attn_sw(qkv_s3, csw_k, csw_v, sw_sink, l).reshape(nb_lat * T, 512)
        oc = _attn_da(qkv_s, cda_k, cda_v, da_lambda, subg, bd128, l, lam_init)
        xs = _ffn(xs, oa, ob, oc, mod4, lat_row, l, g_ffn, w_out_bf, w_up_bf, conv_w, conv_b, w_down_bf, T)

    def heads(a, tail):
        return a.reshape((nb_ctx, DEPTH, seq) + tail)

    return (xp.reshape(nb_ctx, seq, D_MODEL), xs.reshape(nb_lat, T, D_MODEL),
            heads(caches[0], (NA_HEADS, HEAD_DIM)), heads(caches[1], (NA_HEADS, HEAD_DIM)),
            heads(caches[2], (SW_KV_HEADS, HEAD_DIM)), heads(caches[3], (SW_KV_HEADS, HEAD_DIM)),
            heads(caches[4], (DA_HEADS, 2, DA_DIM)), heads(caches[5], (DA_HEADS, DA_DIM * 2)))
```

```python
import functools
import math

import numpy as np
import jax
import jax.numpy as jnp
from jax import lax
from jax.experimental import pallas as pl
from jax.experimental.pallas import tpu as pltpu

F32 = jnp.float32
BF16 = jnp.bfloat16

D_MODEL = 1024
DEPTH = 4
GRID_W = 64
HEAD_DIM = 64
NA_HEADS = 4
NA_WIN_ROWS = 8
NA_WIN_COLS = 16
SW_HEADS = 8
SW_KV_HEADS = 2
SW_WINDOW = 128
DA_HEADS = 4
DA_DIM = 32
D_FF = 2816
ROPE_BASE = 10000.0
EPS = 1e-6

QB = 0
QA, KA, VA = 512, 768, 1024
KB, VB = 1280, 1408
QC, KC, VC = 1536, 1792, 2048
D_IN = 2304
D_MIX = 1024
_SW_ORDER = (0, 4, 1, 5, 2, 6, 3, 7)

LANES = 128
NEG = -1e30
LOG2E = math.log2(math.e)
TQ = 256
TQ_DA = 1024
CTX_SEQS = 4
LAT_BATCHES = 4
TM_IN = 1024
TM_FFN = 512
HALO = 16
CK_FFN = 256
FFN_STAGE_ROWS = 256
VMEM_LIMIT = 56 * 1024 * 1024


def _cparams(sem):
    return pltpu.CompilerParams(dimension_semantics=sem, vmem_limit_bytes=VMEM_LIMIT)


def _dot(a, b):
    return jnp.dot(a, b, preferred_element_type=F32)


def _dot_nt(a, b):
    return lax.dot_general(a, b, (((1,), (1,)), ((), ())), preferred_element_type=F32)


def _sigmoid(x):
    return 1.0 / (1.0 + jnp.exp(-x))


def _mod_kernel(c_ref, w_ref, b_ref, o_ref):
    c = c_ref[...]
    s = (c * _sigmoid(c)).astype(BF16)
    o_ref[...] = _dot(s, w_ref[...].astype(BF16)) + b_ref[...]


def _modulation(cvecs, w_mod, b_mod):
    nchunk = 4
    cn = 6 * D_MODEL // nchunk
    return pl.pallas_call(
        _mod_kernel,
        grid=(DEPTH, nchunk),
        in_specs=[
            pl.BlockSpec((8, D_MODEL), lambda l, n: (0, 0)),
            pl.BlockSpec((None, D_MODEL, cn), lambda l, n: (l, 0, n)),
            pl.BlockSpec((None, 1, cn), lambda l, n: (l, 0, n)),
        ],
        out_specs=pl.BlockSpec((None, 8, cn), lambda l, n: (l, 0, n)),
        out_shape=jax.ShapeDtypeStruct((DEPTH, 8, 6 * D_MODEL), F32),
        compiler_params=_cparams(("arbitrary", "arbitrary")),
        name="modulation",
    )(cvecs, w_mod, b_mod.reshape(DEPTH, 1, 6 * D_MODEL))


N_RPB_COLS = 2 * NA_WIN_COLS - 1
_BIAS_KINDS = (
    (3, lambda i, j: j >= i),
    (7, lambda i, j: True),
    (11, lambda i, j: j < i),
    (3, lambda i, j: True),
    (11, lambda i, j: True),
    (0, lambda i, j: False),
)
K_PREV, K_CUR, K_NEXT, K_PREV_FULL, K_NEXT_FULL, K_NONE = range(6)


def _bias_kernel(rpb_ref, out_ref):
    lh = pl.program_id(0)
    qcol = lax.broadcasted_iota(jnp.int32, (GRID_W, LANES), 0)
    lane = lax.broadcasted_iota(jnp.int32, (GRID_W, LANES), 1)
    kcol = lane & (GRID_W - 1)
    hi = lane >= GRID_W
    f = jnp.clip(kcol - qcol, -(NA_WIN_COLS - 1), NA_WIN_COLS - 1) + (NA_WIN_COLS - 1)
    qs = jnp.clip(qcol - NA_WIN_COLS // 2, 0, GRID_W - NA_WIN_COLS)
    cvalid = (kcol >= qs) & (kcol < qs + NA_WIN_COLS)
    masked = jnp.full((GRID_W, LANES), NEG, F32)

    @functools.lru_cache(maxsize=None)
    def block_pair(a0, a1):
        if a0 is None and a1 is None:
            return masked
        acc = jnp.zeros((GRID_W, LANES), F32)
        for b in range(N_RPB_COLS):
            s0 = rpb_ref[lh, a0 * N_RPB_COLS + b] if a0 is not None else 0.0
            s1 = rpb_ref[lh, a1 * N_RPB_COLS + b] if a1 is not None else 0.0
            acc = jnp.where(f == b, jnp.where(hi, s1, s0), acc)
        valid = cvalid
        if a0 is None:
            valid = valid & hi
        if a1 is None:
            valid = valid & jnp.logical_not(hi)
        return jnp.where(valid, acc * LOG2E, masked)

    for kind, (off, inside) in enumerate(_BIAS_KINDS):
        for i in range(4):
            for jt in range(2):
                a = [(j - i + off) if inside(i, j) else None for j in (2 * jt, 2 * jt + 1)]
                out_ref[kind, i * GRID_W:(i + 1) * GRID_W, jt * LANES:(jt + 1) * LANES] = block_pair(a[0], a[1])


def _na_bias_tiles(na_rpb):
    n = DEPTH * NA_HEADS
    rpb2 = na_rpb.reshape(n, (2 * NA_WIN_ROWS - 1) * N_RPB_COLS)
    out = pl.pallas_call(
        _bias_kernel,
        grid=(n,),
        in_specs=[pl.BlockSpec(memory_space=pltpu.SMEM)],
        out_specs=pl.BlockSpec((None, len(_BIAS_KINDS), TQ, TQ), lambda i: (i, 0, 0, 0)),
        out_shape=jax.ShapeDtypeStruct((n, len(_BIAS_KINDS), TQ, TQ), F32),
        compiler_params=_cparams(("arbitrary",)),
        name="na_bias_tiles",
    )(rpb2)
    return out.reshape(DEPTH, NA_HEADS, len(_BIAS_KINDS), TQ, TQ)


_NORM64 = ((QB, 256), (QB + 256, 256), (QA, 256), (KA, 256), (KB, 128))
_NORM32 = ((QC, 256), (KC, 256))
_PLAIN = ((VA, 256), (VB, 128), (VC, 256))
_ROPE_B = (QB, QB + 256, KB)
_ROPE_C = (QC, KC)
_CACHE_COLS = (KA, VA, KB, VB, KC, VC)


def _inproj_kernel(*refs, rope, n_alias):
    x_ref, mod_ref, g_ref, w_ref, gvec_ref, bd64_ref, bd32_ref = refs[:7]
    rest = refs[7:]
    if rope:
        cosb_ref, sinb_ref, cosc_ref, sinc_ref = rest[:4]
        rest = rest[4:]
    rest = rest[n_alias:]
    qkv_ref = rest[0]
    cache_refs = rest[1:]

    x = x_ref[...]
    ms = jnp.mean(x * x, axis=-1, keepdims=True)
    y = x * lax.rsqrt(ms + EPS) * g_ref[...]
    h = (y * (1.0 + mod_ref[:, D_MODEL:2 * D_MODEL]) + mod_ref[:, 0:D_MODEL]).astype(BF16)
    tm = x.shape[0]
    even = (lax.broadcasted_iota(jnp.int32, (tm, LANES), 1) & 1) == 0

    def rotary(t, cos_ref, sin_ref):
        cos = cos_ref[...]
        sin = sin_ref[...]
        parts = []
        for c in range(t.shape[1] // LANES):
            tc = t[:, c * LANES:(c + 1) * LANES]
            partner = jnp.where(even, pltpu.roll(tc, LANES - 1, 1), pltpu.roll(tc, 1, 1))
            parts.append(tc * cos + partner * sin)
        return parts[0] if len(parts) == 1 else jnp.concatenate(parts, axis=1)

    cache_of = dict(zip(_CACHE_COLS, cache_refs))

    def emit(a, w, t):
        qkv_ref[:, a:a + w] = t.astype(BF16)
        if a in cache_of:
            ref = cache_of[a]
            seq = ref.shape[1]
            for n in range(ref.shape[0]):
                ref[n] = t[n * seq:(n + 1) * seq]

    p = _dot(h, w_ref[...])
    for segs, bd_ref in ((_NORM64, bd64_ref), (_NORM32, bd32_ref)):
        for a, w in segs:
            t = p[:, a:a + w]
            msq = _dot((t * t).astype(BF16), bd_ref[0:w, 0:w])
            t = t * lax.rsqrt(msq + EPS) * gvec_ref[:, a:a + w]
            if rope and a in _ROPE_B:
                t = rotary(t, cosb_ref, sinb_ref)
            if rope and a in _ROPE_C:
                t = rotary(t, cosc_ref, sinc_ref)
            emit(a, w, t)
    for a, w in _PLAIN:
        emit(a, w, p[:, a:a + w])


def _inproj(x, mod4, mod_row_fn, l, g_attn, w_in_bf, gvec, bd64, bd32, rope_tabs, caches, seq):
    rows = x.shape[0]
    tm = TM_IN
    grid = (rows // tm,)
    rope = rope_tabs is not None
    in_specs = [
        pl.BlockSpec((tm, D_MODEL), lambda i: (i, 0)),
        pl.BlockSpec((None, None, 1, 6 * D_MODEL), lambda i: (l, mod_row_fn(i, tm), 0, 0)),
        pl.BlockSpec((None, 1, D_MODEL), lambda i: (l, 0, 0)),
        pl.BlockSpec((None, D_MODEL, D_IN), lambda i: (l, 0, 0), pipeline_mode=pl.Buffered(1)),
        pl.BlockSpec((None, 1, D_IN), lambda i: (l, 0, 0)),
        pl.BlockSpec((256, 256), lambda i: (0, 0)),
        pl.BlockSpec((256, 256), lambda i: (0, 0)),
    ]
    args = [x, mod4, g_attn.reshape(DEPTH, 1, D_MODEL), w_in_bf, gvec, bd64, bd32]
    if rope:
        nt = rope_tabs[0].shape[0] // tm
        in_specs += [pl.BlockSpec((tm, LANES), lambda i: (i % nt, 0))] * 4
        args += list(rope_tabs)
    out_shape = [jax.ShapeDtypeStruct((rows, D_IN), BF16)]
    out_specs = [pl.BlockSpec((tm, D_IN), lambda i: (i, 0))]
    aliases = {}
    n_alias = 0
    if caches is not None:
        per = tm // seq
        n_alias = len(caches)
        for n, w in enumerate((256, 256, 128, 128, 256, 256)):
            out_shape.append(jax.ShapeDtypeStruct((rows // seq, DEPTH, seq, w), F32))
            out_specs.append(pl.BlockSpec((per, None, seq, w), lambda i: (i, l, 0, 0)))
            if caches:
                aliases[len(args) + n] = 1 + n
        in_specs += [pl.BlockSpec(memory_space=pl.ANY)] * n_alias
        args += list(caches)
    return pl.pallas_call(
        functools.partial(_inproj_kernel, rope=rope, n_alias=n_alias),
        grid=grid,
        in_specs=in_specs,
        out_specs=out_specs,
        out_shape=out_shape,
        input_output_aliases=aliases,
        compiler_params=_cparams(("parallel",)),
        name="inproj_lat" if rope else "inproj_ctx",
    )(*args)


def _lane_masks():
    lane = lax.broadcasted_iota(jnp.int32, (1, LANES), 1)
    lo = lane < HEAD_DIM
    quarters = [(lane >= DA_DIM * j) & (lane < DA_DIM * (j + 1)) for j in range(4)]
    return lo, quarters


def _keep(q, mask):
    return jnp.where(mask, q, jnp.zeros_like(q))


def _attend(qm, ks, vs, biases=None, sink=None):
    if sink is not None:
        sink = sink * LOG2E
    ss = []
    for n, k in enumerate(ks):
        s = _dot_nt(qm, k)
        if biases is not None and biases[n] is not None:
            s = s + biases[n]
        ss.append(s)
    m = ss[0].max(axis=-1, keepdims=True)
    for s in ss[1:]:
        m = jnp.maximum(m, s.max(axis=-1, keepdims=True))
    if sink is not None:
        m = jnp.maximum(m, sink)
    l = None
    o = None
    for s, v in zip(ss, vs):
        p = jnp.exp2(s - m)
        ls = p.sum(axis=-1, keepdims=True)
        os = _dot(p.astype(BF16), v)
        l = ls if l is None else l + ls
        o = os if o is None else o + os
    if sink is not None:
        l = l + jnp.exp2(sink - m)
    return o * (1.0 / l)


def _diff_lambda(lam_ref, lam_init):
    lp = lam_ref[...]
    a = jnp.sum(lp[0:1] * lp[1:2], axis=-1, keepdims=True)
    b = jnp.sum(lp[2:3] * lp[3:4], axis=-1, keepdims=True)
    return jnp.exp(a) - jnp.exp(b) + lam_init


def _diff_pair(q, ks, vs, lam, subg, bd, lam_init, lo, quarters):
    o = [_attend(_keep(q, quarters[j]), ks, vs) for j in range(4)]
    d = jnp.where(lo, o[0] - lam * o[1], o[2] - lam * o[3])
    msq = _dot((d * d).astype(BF16), bd)
    return d * lax.rsqrt(msq + EPS) * subg * (1.0 - lam_init)


def _attn_ctx_kernel(qkv_ref, sink_ref, lam_ref, subg_ref, bd_ref, oa_ref, ob_ref, oc_ref, *, lam_init):
    lo, quarters = _lane_masks()
    hi = jnp.logical_not(lo)
    lam = _diff_lambda(lam_ref, lam_init)
    for n in range(qkv_ref.shape[0] // TQ):
        rows = slice(n * TQ, (n + 1) * TQ)
        for pr in range(2):
            c = pr * LANES
            q = qkv_ref[rows, QA + c:QA + c + LANES]
            k = [qkv_ref[rows, KA + c:KA + c + LANES]]
            v = [qkv_ref[rows, VA + c:VA + c + LANES]]
            o = jnp.where(lo, _attend(_keep(q, lo), k, v), _attend(_keep(q, hi), k, v))
            oa_ref[rows, c:c + LANES] = o.astype(BF16)
        k = [qkv_ref[rows, KB:KB + LANES]]
        v = [qkv_ref[rows, VB:VB + LANES]]
        for pr in range(4):
            c = pr * LANES
            q = qkv_ref[rows, QB + c:QB + c + LANES]
            o0 = _attend(_keep(q, lo), k, v, sink=sink_ref[_SW_ORDER[2 * pr]])
            o1 = _attend(_keep(q, hi), k, v, sink=sink_ref[_SW_ORDER[2 * pr + 1]])
            ob_ref[rows, c:c + LANES] = jnp.where(lo, o0, o1).astype(BF16)
        for pr in range(2):
            c = pr * LANES
            q = qkv_ref[rows, QC + c:QC + c + LANES]
            k = [qkv_ref[rows, KC + c:KC + c + LANES]]
            v = [qkv_ref[rows, VC + c:VC + c + LANES]]
            o = _diff_pair(q, k, v, lam, subg_ref[...], bd_ref[...], lam_init, lo, quarters)
            oc_ref[rows, c:c + LANES] = o.astype(BF16)


def _attn_ctx(qkv, l, sw_sink, da_lambda, subg, bd128, lam_init):
    rows = qkv.shape[0]
    tr = CTX_SEQS * TQ
    return pl.pallas_call(
        functools.partial(_attn_ctx_kernel, lam_init=lam_init),
        grid=(rows // tr,),
        in_specs=[
            pl.BlockSpec((tr, D_IN), lambda b: (b, 0)),
            pl.BlockSpec(memory_space=pltpu.SMEM),
            pl.BlockSpec((None, 4, DA_DIM), lambda b: (l, 0, 0)),
            pl.BlockSpec((None, 1, LANES), lambda b: (l, 0, 0)),
            pl.BlockSpec((LANES, LANES), lambda b: (0, 0)),
        ],
        out_specs=[
            pl.BlockSpec((tr, 256), lambda b: (b, 0)),
            pl.BlockSpec((tr, 512), lambda b: (b, 0)),
            pl.BlockSpec((tr, 256), lambda b: (b, 0)),
        ],
        out_shape=[
            jax.ShapeDtypeStruct((rows, 256), BF16),
            jax.ShapeDtypeStruct((rows, 512), BF16),
            jax.ShapeDtypeStruct((rows, 256), BF16),
        ],
        compiler_params=_cparams(("parallel",)),
        name="attn_ctx",
    )(qkv, sw_sink[l], da_lambda, subg, bd128)


NT_LAT = 2048 // TQ


def _attn_na_kernel(q_ref, kp_ref, kc_ref, kn_ref, vp_ref, vc_ref, vn_ref, ck_ref, cv_ref,
                    bp_ref, bc_ref, bn_ref, o_ref):
    lo, _ = _lane_masks()
    hi = jnp.logical_not(lo)
    for n in range(q_ref.shape[0]):
        ck = ck_ref[n].astype(BF16)
        cv = cv_ref[n].astype(BF16)
        for pr in range(2):
            c = pr * LANES
            sl = slice(c, c + LANES)
            q = q_ref[n, :, sl]
            ks = [kp_ref[n, :, sl], kc_ref[n, :, sl], kn_ref[n, :, sl], ck[:, sl]]
            vs = [vp_ref[n, :, sl], vc_ref[n, :, sl], vn_ref[n, :, sl], cv[:, sl]]
            outs = []
            for hl, mask in enumerate((lo, hi)):
                h = 2 * pr + hl
                outs.append(_attend(_keep(q, mask), ks, vs, biases=[bp_ref[h], bc_ref[h], bn_ref[h], None]))
            o_ref[n, :, sl] = jnp.where(lo, outs[0], outs[1]).astype(BF16)


def _attn_na(qkv, cache_k, cache_v, bias, l):
    nbatch, _, _ = qkv.shape
    nb = LAT_BATCHES
    last = NT_LAT - 1

    def kv_specs(col):
        return [
            pl.BlockSpec((nb, TQ, 256), lambda r, b: (b, jnp.maximum(r - 1, 0), col)),
            pl.BlockSpec((nb, TQ, 256), lambda r, b: (b, r, col)),
            pl.BlockSpec((nb, TQ, 256), lambda r, b: (b, jnp.minimum(r + 1, last), col)),
        ]

    def kind_prev(r):
        return jnp.where(r == 0, K_NONE, jnp.where(r == last, K_PREV_FULL, K_PREV))

    def kind_next(r):
        return jnp.where(r == 0, K_NEXT_FULL, jnp.where(r == last, K_NONE, K_NEXT))

    def bias_spec(kind_fn):
        return pl.BlockSpec((None, NA_HEADS, None, TQ, TQ), lambda r, b: (l, 0, kind_fn(r), 0, 0))

    cache_spec = pl.BlockSpec((nb, None, 512, 256), lambda r, b: (b, l, 0, 0))
    return pl.pallas_call(
        _attn_na_kernel,
        grid=(NT_LAT, nbatch // nb),
        in_specs=[pl.BlockSpec((nb, TQ, 256), lambda r, b: (b, r, QA // 256))]
        + kv_specs(KA // 256) + kv_specs(VA // 256)
        + [cache_spec, cache_spec,
           bias_spec(kind_prev), bias_spec(lambda r: K_CUR), bias_spec(kind_next)],
        out_specs=pl.BlockSpec((nb, TQ, 256), lambda r, b: (b, r, 0)),
        out_shape=jax.ShapeDtypeStruct((nbatch, NT_LAT * TQ, 256), BF16),
        compiler_params=_cparams(("parallel", "parallel")),
        name="attn_lat_na",
    )(qkv, qkv, qkv, qkv, qkv, qkv, qkv, cache_k, cache_v, bias, bias, bias)


def _attn_sw_kernel(q_ref, k0_ref, k1_ref, k2_ref, k3_ref, v0_ref, v1_ref, v2_ref, v3_ref,
                    ck_ref, cv_ref, sink_ref, o_ref):
    r = pl.program_id(1)
    lo, _ = _lane_masks()
    hi = jnp.logical_not(lo)
    qi = lax.broadcasted_iota(jnp.int32, (TQ, TQ), 0)
    kj = lax.broadcasted_iota(jnp.int32, (TQ, TQ), 1)
    zero = jnp.zeros((TQ, TQ), F32)
    neg = jnp.full((TQ, TQ), NEG, F32)
    bias_a = jnp.where((kj >= qi) & ((r > 0) | (kj >= SW_WINDOW)), zero, neg)
    bias_b = jnp.where((kj <= qi) & ((r < NT_LAT - 1) | (kj < SW_WINDOW)), zero, neg)
    biases = [bias_a, bias_b, None]
    for n in range(q_ref.shape[0]):
        ks = [jnp.concatenate([k0_ref[n], k1_ref[n]], axis=0),
              jnp.concatenate([k2_ref[n], k3_ref[n]], axis=0), ck_ref[n].astype(BF16)]
        vs = [jnp.concatenate([v0_ref[n], v1_ref[n]], axis=0),
              jnp.concatenate([v2_ref[n], v3_ref[n]], axis=0), cv_ref[n].astype(BF16)]
        for pr in range(4):
            c = pr * LANES
            q = q_ref[n, :, c:c + LANES]
            o0 = _attend(_keep(q, lo), ks, vs, biases=biases, sink=sink_ref[_SW_ORDER[2 * pr]])
            o1 = _attend(_keep(q, hi), ks, vs, biases=biases, sink=sink_ref[_SW_ORDER[2 * pr + 1]])
            o_ref[n, :, c:c + LANES] = jnp.where(lo, o0, o1).astype(BF16)


def _attn_sw(qkv, cache_k, cache_v, sw_sink, l):
    nbatch, T, _ = qkv.shape
    nb = LAT_BATCHES
    nblk = T // SW_WINDOW

    def kv_specs(col):
        return [pl.BlockSpec((nb, SW_WINDOW, LANES),
                             lambda b, r, t=t: (b, jnp.clip(2 * r - 1 + t, 0, nblk - 1), col))
                for t in range(4)]

    cache_spec = pl.BlockSpec((nb, None, 512, LANES), lambda b, r: (b, l, 0, 0))
    return pl.pallas_call(
        _attn_sw_kernel,
        grid=(nbatch // nb, NT_LAT),
        in_specs=[pl.BlockSpec((nb, TQ, 512), lambda b, r: (b, r, QB // 512))]
        + kv_specs(KB // LANES) + kv_specs(VB // LANES)
        + [cache_spec, cache_spec, pl.BlockSpec(memory_space=pltpu.SMEM)],
        out_specs=pl.BlockSpec((nb, TQ, 512), lambda b, r: (b, r, 0)),
        out_shape=jax.ShapeDtypeStruct((nbatch, T, 512), BF16),
        compiler_params=_cparams(("parallel", "parallel")),
        name="attn_lat_sw",
    )(*([qkv] * 9), cache_k, cache_v, sw_sink[l])


def _attn_da_kernel(q_ref, k_ref, v_ref, ck_ref, cv_ref, lam_ref, subg_ref, bd_ref, o_ref, *, lam_init):
    lo, quarters = _lane_masks()
    lam = _diff_lambda(lam_ref, lam_init)
    ks = [k_ref[...], ck_ref[...].astype(BF16)]
    vs = [v_ref[...], cv_ref[...].astype(BF16)]
    o = _diff_pair(q_ref[...], ks, vs, lam, subg_ref[...], bd_ref[...], lam_init, lo, quarters)
    o_ref[...] = o.astype(BF16)


def _attn_da(qkv, cache_k, cache_v, da_lambda, subg, bd128, l, lam_init):
    rows = qkv.shape[0]
    nbatch = rows // 2048
    cache_spec = pl.BlockSpec((None, None, 512, LANES), lambda b, p, r: (b, l, 0, p))
    nt = 2048 // TQ_DA
    return pl.pallas_call(
        functools.partial(_attn_da_kernel, lam_init=lam_init),
        grid=(nbatch, 2, nt),
        in_specs=[
            pl.BlockSpec((TQ_DA, LANES), lambda b, p, r: (b * nt + r, QC // LANES + p)),
            pl.BlockSpec((2048, LANES), lambda b, p, r: (b, KC // LANES + p)),
            pl.BlockSpec((2048, LANES), lambda b, p, r: (b, VC // LANES + p)),
            cache_spec, cache_spec,
            pl.BlockSpec((None, 4, DA_DIM), lambda b, p, r: (l, 0, 0)),
            pl.BlockSpec((None, 1, LANES), lambda b, p, r: (l, 0, 0)),
            pl.BlockSpec((LANES, LANES), lambda b, p, r: (0, 0)),
        ],
        out_specs=pl.BlockSpec((TQ_DA, LANES), lambda b, p, r: (b * nt + r, p)),
        out_shape=jax.ShapeDtypeStruct((rows, 256), BF16),
        compiler_params=_cparams(("parallel", "parallel", "parallel")),
        name="attn_lat_da",
    )(qkv, qkv, qkv, cache_k, cache_v, da_lambda, subg, bd128)


def _ffn_kernel(x_ref, xp_ref, xn_ref, oa_ref, oap_ref, oan_ref, ob_ref, obp_ref, obn_ref,
                oc_ref, ocp_ref, ocn_ref, mod_ref, g_ref, wo_ref, wup_ref, cw_ref, cb_ref, wd_ref,
                out_ref, x1_sc, h2_sc, gated_sc, ug0_sc, uv0_sc, ug1_sc, uv1_sc, *, seqlen):
    i = pl.program_id(0)
    tm = x_ref.shape[0]

    def prepare():
        gate1 = mod_ref[:, 2 * D_MODEL:3 * D_MODEL]
        shift2 = mod_ref[:, 3 * D_MODEL:4 * D_MODEL]
        scale2 = mod_ref[:, 4 * D_MODEL:5 * D_MODEL]

        def stage(xv, oa, ob, oc):
            mix = _dot(oa, wo_ref[0:256]) + _dot(ob, wo_ref[256:768]) + _dot(oc, wo_ref[768:1024])
            x1 = xv + gate1 * mix
            ms = jnp.mean(x1 * x1, axis=-1, keepdims=True)
            y = x1 * lax.rsqrt(ms + EPS) * g_ref[...]
            return x1, (y * (1.0 + scale2) + shift2).astype(BF16)

        for r0 in range(0, tm, FFN_STAGE_ROWS):
            r1 = r0 + FFN_STAGE_ROWS
            x1, h2 = stage(x_ref[r0:r1], oa_ref[r0:r1], ob_ref[r0:r1], oc_ref[r0:r1])
            x1_sc[r0:r1] = x1
            h2_sc[HALO + r0:HALO + r1] = h2
        h2_sc[0:HALO] = stage(xp_ref[...], oap_ref[...], obp_ref[...], ocp_ref[...])[1]
        h2_sc[HALO + tm:2 * HALO + tm] = stage(xn_ref[...], oan_ref[...], obn_ref[...], ocn_ref[...])[1]

    prepare()
    h2 = h2_sc[...]
    ck = CK_FFN
    mtot = tm + 2 * HALO
    sub = 8
    span = min(seqlen, tm)
    row8 = lax.broadcasted_iota(jnp.int32, (sub, ck), 0)

    def zero_rows(v, group_starts, edge):
        pieces = []
        cur = 0
        for g0 in group_starts:
            if g0 > cur:
                pieces.append(v[cur:g0])
            pos = (i * tm + g0 + row8) & (seqlen - 1)
            pieces.append(jnp.where(pos != edge, v[g0:g0 + sub], 0.0))
            cur = g0 + sub
        if cur < tm:
            pieces.append(v[cur:])
        return jnp.concatenate(pieces, axis=0)

    first_groups = list(range(0, tm, span))
    last_groups = [g + span - sub for g in first_groups]

    def conv(u, col):
        cw = cw_ref[:, col:col + ck]
        up = zero_rows(pltpu.roll(u, 1, 0)[HALO:HALO + tm], first_groups, 0)
        un = zero_rows(pltpu.roll(u, mtot - 1, 0)[HALO:HALO + tm], last_groups, seqlen - 1)
        y = cb_ref[:, col:col + ck] + up * cw[0:1]
        y = y + u[HALO:HALO + tm] * cw[1:2]
        return y + un * cw[2:3]

    def silu_mul(g, v):
        h = 0.5 * g
        return (h * jnp.tanh(h) + h) * v

    nj = D_FF // ck
    bufs = ((ug0_sc, uv0_sc), (ug1_sc, uv1_sc))

    def up_project(c):
        ug_sc, uv_sc = bufs[c % 2]
        ug_sc[...] = _dot(h2, wup_ref[:, c * ck:(c + 1) * ck])
        uv_sc[...] = _dot(h2, wup_ref[:, D_FF + c * ck:D_FF + (c + 1) * ck])

    def gate(c):
        ug_sc, uv_sc = bufs[c % 2]
        yg = conv(ug_sc[...], c * ck)
        yv = conv(uv_sc[...], D_FF + c * ck)
        gated_sc[:, c * ck:(c + 1) * ck] = silu_mul(yg, yv).astype(BF16)

    up_project(0)
    for c in range(nj):
        if c + 1 < nj:
            up_project(c + 1)
        gate(c)
    ffn = _dot(gated_sc[...], wd_ref[...])
    out_ref[...] = x1_sc[...] + mod_ref[:, 5 * D_MODEL:6 * D_MODEL] * ffn


def _ffn(x, oa, ob, oc, mod4, mod_row_fn, l, g_ffn, w_out_bf, w_up_bf, conv_w, conv_b, w_down_bf, seqlen):
    rows = x.shape[0]
    tm = TM_FFN
    nhb = rows // HALO
    per = tm // HALO

    def tile(w):
        return pl.BlockSpec((tm, w), lambda i: (i, 0))

    def prev(w):
        return pl.BlockSpec((HALO, w), lambda i: (jnp.maximum(i * per - 1, 0), 0))

    def nxt(w):
        return pl.BlockSpec((HALO, w), lambda i: (jnp.minimum((i + 1) * per, nhb - 1), 0))

    def trio(w):
        return [tile(w), prev(w), nxt(w)]

    def resident(shape):
        return pl.BlockSpec((None,) + shape, lambda i: (l, 0, 0), pipeline_mode=pl.Buffered(1))

    in_specs = trio(D_MODEL) + trio(256) + trio(512) + trio(256) + [
        pl.BlockSpec((None, None, 1, 6 * D_MODEL), lambda i: (l, mod_row_fn(i, tm), 0, 0)),
        resident((1, D_MODEL)),
        resident((D_MIX, D_MODEL)),
        resident((D_MODEL, 2 * D_FF)),
        resident((3, 2 * D_FF)),
        resident((1, 2 * D_FF)),
        resident((D_FF, D_MODEL)),
    ]
    return pl.pallas_call(
        functools.partial(_ffn_kernel, seqlen=seqlen),
        grid=(rows // tm,),
        in_specs=in_specs,
        out_specs=pl.BlockSpec((tm, D_MODEL), lambda i: (i, 0)),
        out_shape=jax.ShapeDtypeStruct((rows, D_MODEL), F32),
        scratch_shapes=[
            pltpu.VMEM((tm, D_MODEL), F32),
            pltpu.VMEM((tm + 2 * HALO, D_MODEL), BF16),
            pltpu.VMEM((tm, D_FF), BF16),
        ] + [pltpu.VMEM((tm + 2 * HALO, CK_FFN), F32)] * 4,
        compiler_params=_cparams(("parallel",)),
        name="outproj_ffn",
    )(x, x, x, oa, oa, oa, ob, ob, ob, oc, oc, oc, mod4, g_ffn.reshape(DEPTH, 1, D_MODEL),
      w_out_bf, w_up_bf, conv_w, conv_b.reshape(DEPTH, 1, 2 * D_FF), w_down_bf)


def _block_diag_mean(n, group):
    idx = np.arange(n) // group
    return jnp.asarray((idx[:, None] == idx[None, :]).astype(np.float32) / group, dtype=BF16)


def _rope_tables(T, dim):
    n = dim // 4
    inv = 1.0 / (ROPE_BASE ** (jnp.arange(n, dtype=F32) / n))
    t = jnp.arange(T)
    rowp = (t // GRID_W).astype(F32)
    colp = (t % GRID_W).astype(F32)
    ang = jnp.concatenate([rowp[:, None] * inv, colp[:, None] * inv], axis=-1)
    cos = jnp.repeat(jnp.cos(ang), 2, axis=-1)
    sin = jnp.repeat(jnp.sin(ang), 2, axis=-1) * jnp.tile(jnp.asarray([-1.0, 1.0], F32), dim // 2)
    reps = LANES // dim
    return jnp.tile(cos, (1, reps)), jnp.tile(sin, (1, reps))


def _pair_sw_heads(w, axis):
    assert _SW_ORDER == tuple(g * 4 + i for i in range(4) for g in range(2))
    shp = w.shape
    w = w.reshape(shp[:axis] + (2, 4, HEAD_DIM) + shp[axis + 1:])
    return jnp.swapaxes(w, axis, axis + 1).reshape(shp)


def kernel(x_prompt, x_sample, cache_na_k, cache_na_v, cache_sw_k, cache_sw_v, cache_da_k, cache_da_v, c, c_ctx, g_attn, g_ffn, w_mod, b_mod, w_in, na_qk_g, na_rpb, sw_qk_g, sw_sink, da_qk_g, da_lambda, da_subln_g, w_out, w_up, conv_w, conv_b, w_down):
    nb_ctx, seq, _ = x_prompt.shape
    nb_lat, T, _ = x_sample.shape
    npast = cache_na_k.shape[2]

    w_in_bf = jnp.concatenate([
        _pair_sw_heads(w_in[:, :, 768:1280], 2), w_in[:, :, 0:768], w_in[:, :, 1280:]], axis=2).astype(BF16)
    w_out_bf = jnp.concatenate([
        w_out[:, 0:256], _pair_sw_heads(w_out[:, 256:768], 1), w_out[:, 768:]], axis=1).astype(BF16)
    w_up_bf = w_up.astype(BF16)
    w_down_bf = w_down.astype(BF16)

    ones = lambda n: jnp.ones((DEPTH, n), F32)
    sc64 = HEAD_DIM ** -0.5 * LOG2E
    sc32 = DA_DIM ** -0.5 * LOG2E
    gvec = jnp.concatenate([
        jnp.tile(sw_qk_g[:, 0], (1, 8)) * sc64,
        jnp.tile(na_qk_g[:, 0], (1, 4)) * sc64, jnp.tile(na_qk_g[:, 1], (1, 4)), ones(256),
        jnp.tile(sw_qk_g[:, 1], (1, 2)), ones(128),
        jnp.tile(da_qk_g[:, 0], (1, 8)) * sc32, jnp.tile(da_qk_g[:, 1], (1, 8)), ones(256),
    ], axis=-1).reshape(DEPTH, 1, D_IN)
    subg = jnp.tile(da_subln_g, (1, 2)).reshape(DEPTH, 1, LANES)
    bd64 = _block_diag_mean(256, HEAD_DIM)
    bd32 = _block_diag_mean(256, DA_DIM)
    bd128 = _block_diag_mean(LANES, HEAD_DIM)
    rope_tabs = _rope_tables(T, HEAD_DIM) + _rope_tables(T, DA_DIM)

    cna_k = cache_na_k.reshape(nb_lat, DEPTH, npast, 256)
    cna_v = cache_na_v.reshape(nb_lat, DEPTH, npast, 256)
    csw_k = cache_sw_k.reshape(nb_lat, DEPTH, npast, 128)
    csw_v = cache_sw_v.reshape(nb_lat, DEPTH, npast, 128)
    cda_k = cache_da_k.reshape(nb_lat, DEPTH, npast, 256)
    cda_v = cache_da_v.reshape(nb_lat, DEPTH, npast, 256)

    cvecs = jnp.concatenate([c_ctx[None, :], c, jnp.zeros((8 - 1 - nb_lat, D_MODEL), F32)], axis=0)
    mod4 = _modulation(cvecs, w_mod, b_mod).reshape(DEPTH, 8, 1, 6 * D_MODEL)
    bias = _na_bias_tiles(na_rpb)

    ctx_row = lambda i, tm: 0
    lat_row = lambda i, tm: 1 + (i * tm) // T

    xp = x_prompt.reshape(nb_ctx * seq, D_MODEL)
    xs = x_sample.reshape(nb_lat * T, D_MODEL)
    caches = ()
    for l in range(DEPTH):
        lam_init = 0.8 - 0.6 * math.exp(-0.3 * l)
        qkv_p, *caches = _inproj(xp, mod4, ctx_row, l, g_attn, w_in_bf, gvec, bd64, bd32, None, caches, seq)
        oa, ob, oc = _attn_ctx(qkv_p, l, sw_sink, da_lambda, subg, bd128, lam_init)
        xp = _ffn(xp, oa, ob, oc, mod4, ctx_row, l, g_ffn, w_out_bf, w_up_bf, conv_w, conv_b, w_down_bf, seq)
        qkv_s = _inproj(xs, mod4, lat_row, l, g_attn, w_in_bf, gvec, bd64, bd32, rope_tabs, None, T)[0]
        qkv_s3 = qkv_s.reshape(nb_lat, T, D_IN)
        oa = _attn_na(qkv_s3, cna_k, cna_v, bias, l).reshape(nb_lat * T, 256)
        ob = _attn_sw(qkv_s3, csw_k, csw_v, sw_sink, l).reshape(nb_lat * T, 512)
        oc = _attn_da(qkv_s, cda_k, cda_v, da_lambda, subg, bd128, l, lam_init)
        xs = _ffn(xs, oa, ob, oc, mod4, lat_row, l, g_ffn, w_out_bf, w_up_bf, conv_w, conv_b, w_down_bf, T)

    def heads(a, tail):
        return a.reshape((nb_ctx, DEPTH, seq) + tail)

    return (xp.reshape(nb_ctx, seq, D_MODEL), xs.reshape(nb_lat, T, D_MODEL),
            heads(caches[0], (NA_HEADS, HEAD_DIM)), heads(caches[1], (NA_HEADS, HEAD_DIM)),
            heads(caches[2], (SW_KV_HEADS, HEAD_DIM)), heads(caches[3], (SW_KV_HEADS, HEAD_DIM)),
            heads(caches[4], (DA_HEADS, 2, DA_DIM)), heads(caches[5], (DA_HEADS, DA_DIM * 2)))
```
